```python
import jax, jax.numpy as jnp
from jax import lax
import numpy as np

D_MODEL = 2048
BATCH = 8
SEQ = 2048
DEPTH = 2

GRID_W = 64
CTX_LEN = 256
D_CONV = D_MODEL // 2
CONV_W = 3
HG_DK = 128
HG_DV = 128
D_F = D_MODEL // 2
D_V = D_MODEL // 2
HG_HEADS = D_F // HG_DK
CHUNK = 64
PEER_HEADS = 8
N_KEYS = 128
N_EXPERTS = N_KEYS * N_KEYS
D_QUERY = 256
TOPK_HALF = 16
TOPK = 16
PEER_BLOCK = 128
IN_WIDTH = 3 * D_CONV + 3 * D_F + 2 * D_V + 2 * D_MODEL
ALPHA = (2.0 * DEPTH) ** 0.25
BETA = (8.0 * DEPTH) ** -0.25
EPS = 1e-6
F_MIN = 1e-30

kernel_name = 'hybrid_conv_hgrn2_peer_diffusion_block'


def _layer_norm(x, g=None, b=None):
    xf = x.astype(jnp.float32)
    mu = jnp.mean(xf, axis=-1, keepdims=True)
    var = jnp.mean(jnp.square(xf - mu), axis=-1, keepdims=True)
    y = (xf - mu) * lax.rsqrt(var + EPS)
    if g is not None:
        y = y * g.astype(jnp.float32) + b.astype(jnp.float32)
    return y.astype(x.dtype)


def _modulate(x, shift, scale):
    return _layer_norm(x) * (1 + scale) + shift


def _split_in(p):
    sizes = (D_CONV,) * 3 + (D_F,) * 3 + (D_V,) * 2 + (D_MODEL,) * 2
    cuts = [int(s) for s in np.cumsum(sizes)[:-1]]
    return jnp.split(p, cuts, axis=-1)


def _short_conv(u, w, b):
    pad = [(0, 0)] * (u.ndim - 2) + [(1, 1), (0, 0)]
    up = jnp.pad(u, pad)
    return up[..., :-2, :] * w[0] + up[..., 1:-1, :] * w[1] + up[..., 2:, :] * w[2] + b


def _conv_branch(cb, cc, cv, w, b, grid):
    bsz, t, ch = cv.shape
    u = cc * cv
    if grid:
        rows = t // GRID_W
        u = u.reshape(bsz, rows, GRID_W, ch)
    y = _short_conv(u, w, b).reshape(bsz, t, ch)
    return cb * y


def _heads(a, dh):
    bsz, t, _ = a.shape
    return a.reshape(bsz, t, -1, dh).transpose(0, 2, 1, 3)


def _forget_terms(z, lb):
    zf = z.astype(jnp.float32)
    f = lb + (1.0 - lb) * jax.nn.sigmoid(zf)
    logf = jnp.log(jnp.maximum(f, F_MIN))
    k = (1.0 - lb) * jax.nn.sigmoid(-zf)
    return _heads(logf, HG_DK), _heads(k, HG_DK)


def _chunk_scan(q, k, v, logf, s0):
    bsz, nh, t, dk = q.shape
    n = t // CHUNK

    def to_chunks(a):
        return jnp.moveaxis(a.reshape(bsz, nh, n, CHUNK, a.shape[-1]), 2, 0)

    mask = jnp.tril(jnp.ones((CHUNK, CHUNK), dtype=bool))[:, :, None]

    def step(s, inp):
        qb, kb, vb, gb = inp
        bcum = jnp.cumsum(gb.astype(jnp.float32), axis=2)
        diff = bcum[:, :, :, None, :] - bcum[:, :, None, :, :]
        decay = jnp.where(mask, jnp.exp(jnp.where(mask, diff, 0.0)), 0.0)
        scores = jnp.einsum('bhtk,bhsk,bhtsk->bhts', qb, kb, decay)
        o = (jnp.einsum('bhts,bhsv->bhtv', scores, vb)
             + jnp.einsum('bhtk,bhkv->bhtv', qb * jnp.exp(bcum), s))
        blast = bcum[:, :, -1:, :]
        s_new = (jnp.exp(blast[:, :, 0, :])[..., None] * s
                 + jnp.einsum('bhsk,bhsv->bhkv', kb * jnp.exp(blast - bcum), vb))
        return s_new, o

    s_fin, oc = lax.scan(step, s0, (to_chunks(q), to_chunks(k), to_chunks(v), to_chunks(logf)))
    o = jnp.moveaxis(oc, 0, 2).reshape(bsz, nh, t, v.shape[-1])
    return o, s_fin


def _scan_dir(q, k, v, logf, s0, reverse):
    if reverse:
        q, k, v, logf = (jnp.flip(a, axis=2) for a in (q, k, v, logf))
    o, s = _chunk_scan(q, k, v, logf, s0)
    if reverse:
        o = jnp.flip(o, axis=2)
    return o, s


def _hgrn_readout(o, og, norm_g):
    o = o * lax.rsqrt(jnp.mean(jnp.square(o), axis=-1, keepdims=True) + EPS) * norm_g.astype(jnp.float32)
    bsz, nh, t, dv = o.shape
    o = o.transpose(0, 2, 1, 3).reshape(bsz, t, nh * dv)
    return (o * jax.nn.silu(og.astype(jnp.float32))).astype(og.dtype)


def _merge(ya, yb, ga, gb, w_pa, w_pb, w_o):
    m = jax.nn.sigmoid(ga) * (ya @ w_pa) + jax.nn.sigmoid(gb) * (yb @ w_pb)
    return m @ w_o


def _token_mixer(h, hc, w_in, conv_w, conv_b, lb, norm_g, w_pa, w_pb, w_o, ctx_out):
    cb, cc, cv, q, zf, zb, vi, og, ga, gb = _split_in(h @ w_in)
    ccb, ccc, ccv, qc, zfc, zbc, vic, ogc, gac, gbc = _split_in(hc @ w_in)
    qh = _heads(jax.nn.silu(q) * HG_DK ** -0.5, HG_DK)
    vh = _heads(vi, HG_DV)
    qhc = _heads(jax.nn.silu(qc) * HG_DK ** -0.5, HG_DK)
    vhc = _heads(vic, HG_DV)
    s0 = jnp.zeros((h.shape[0], HG_HEADS, HG_DK, HG_DV), jnp.float32)
    outs_lat, outs_ctx = [], []
    for d, (z_lat, z_ctx) in enumerate(((zf, zfc), (zb, zbc))):
        rev = d == 1
        logf_c, k_c = _forget_terms(z_ctx, lb[d])
        o_c, s_c = _scan_dir(qhc, k_c, vhc, logf_c, s0, rev)
        logf_l, k_l = _forget_terms(z_lat, lb[d])
        o_l, _ = _scan_dir(qh, k_l, vh, logf_l, s_c, rev)
        outs_lat.append(o_l)
        outs_ctx.append(o_c)
    y_b = _hgrn_readout(outs_lat[0] + outs_lat[1], og, norm_g)
    y_a = _conv_branch(cb, cc, cv, conv_w, conv_b, grid=True)
    y = _merge(y_a, y_b, ga, gb, w_pa, w_pb, w_o)
    if not ctx_out:
        return y, None
    y_bc = _hgrn_readout(outs_ctx[0] + outs_ctx[1], ogc, norm_g)
    y_ac = _conv_branch(ccb, ccc, ccv, conv_w, conv_b, grid=False)
    yc = _merge(y_ac, y_bc, gac, gbc, w_pa, w_pb, w_o)
    return y, yc


def _peer_route(xf, w_q, keys):
    n = xf.shape[0]
    q = (xf @ w_q).reshape(n, PEER_HEADS, 2, D_QUERY // 2)
    s = jnp.einsum('nhpd,hpkd->nhpk', q, keys).astype(jnp.float32)
    sv, si = lax.top_k(s, TOPK_HALF)
    cand = sv[:, :, 0, :, None] + sv[:, :, 1, None, :]
    cand_idx = si[:, :, 0, :, None] * N_KEYS + si[:, :, 1, None, :]
    top_v, top_pos = lax.top_k(cand.reshape(n, PEER_HEADS, TOPK_HALF * TOPK_HALF), TOPK)
    idx = jnp.take_along_axis(cand_idx.reshape(n, PEER_HEADS, TOPK_HALF * TOPK_HALF), top_pos, axis=-1)
    g = jax.nn.softmax(top_v, axis=-1)
    return idx.reshape(n, PEER_HEADS * TOPK), g.reshape(n, PEER_HEADS * TOPK)


def _peer_ffn(x, w_q, keys, u, v):
    shape = x.shape
    xf = x.reshape(-1, D_MODEL)
    idx, g = _peer_route(xf, w_q, keys)
    nb = xf.shape[0] // PEER_BLOCK

    def block(args):
        xb, ib, gb = args
        ue = u[ib]
        act = jax.nn.gelu(jnp.einsum('nd,nkd->nk', xb, ue).astype(jnp.float32), approximate=False)
        return jnp.einsum('nk,nkd->nd', (gb * act).astype(xb.dtype), v[ib])

    out = lax.map(block, (xf.reshape(nb, PEER_BLOCK, D_MODEL),
                          idx.reshape(nb, PEER_BLOCK, -1),
                          g.reshape(nb, PEER_BLOCK, -1)))
    return out.reshape(shape)


def setup_inputs(seed: int = 0) -> dict:
    key = jax.random.key(seed)
    ks = jax.random.split(key, 22)
    D = D_MODEL

    def nrm(k, shape, s):
        return jax.random.normal(k, shape, jnp.float32) * s

    return {
        'x': nrm(ks[0], (BATCH, SEQ, D), 1.0),
        'c': nrm(ks[1], (BATCH, D), 1.0),
        'ctx': nrm(ks[2], (BATCH, CTX_LEN, D), 1.0),
        'c_ctx': nrm(ks[3], (D,), 1.0),
        'w_mod': nrm(ks[4], (DEPTH, D, 6 * D), 0.5 * D ** -0.5),
        'b_mod': nrm(ks[5], (DEPTH, 6 * D), 0.01),
        'w_in': nrm(ks[6], (DEPTH, D, IN_WIDTH), D ** -0.5),
        'conv_w': nrm(ks[7], (DEPTH, CONV_W, D_CONV), CONV_W ** -0.5),
        'conv_b': nrm(ks[8], (DEPTH, D_CONV), 0.01),
        'lb_raw': 1.0 + nrm(ks[9], (DEPTH, 2, D_F), 0.1),
        'hg_norm_g': 1.0 + nrm(ks[10], (DEPTH, HG_DV), 0.02),
        'w_pa': nrm(ks[11], (DEPTH, D_CONV, D), BETA * D_CONV ** -0.5),
        'w_pb': nrm(ks[12], (DEPTH, D_V, D), BETA * D_V ** -0.5),
        'w_o': nrm(ks[13], (DEPTH, D, D), BETA * D ** -0.5),
        'ln1_g': 1.0 + nrm(ks[14], (DEPTH, D), 0.02),
        'ln1_b': nrm(ks[15], (DEPTH, D), 0.01),
        'peer_wq': nrm(ks[16], (DEPTH, D, PEER_HEADS * D_QUERY), D ** -0.5),
        'peer_keys': nrm(ks[17], (DEPTH, PEER_HEADS, 2, N_KEYS, D_QUERY // 2), (D_QUERY // 2) ** -0.5),
        'peer_u': nrm(ks[18], (DEPTH, N_EXPERTS, D), D ** -0.5),
        'peer_v': nrm(ks[19], (DEPTH, N_EXPERTS, D), BETA * PEER_HEADS ** -0.5),
        'ln2_g': 1.0 + nrm(ks[20], (DEPTH, D), 0.02),
        'ln2_b': nrm(ks[21], (DEPTH, D), 0.01),
    }


def reference(x, c, ctx, c_ctx, w_mod, b_mod, w_in, conv_w, conv_b, lb_raw, hg_norm_g,
              w_pa, w_pb, w_o, ln1_g, ln1_b, peer_wq, peer_keys, peer_u, peer_v, ln2_g, ln2_b):
    p = jax.nn.softmax(lb_raw.astype(jnp.float32), axis=0)
    lower_bounds = jnp.cumsum(p, axis=0) - p[:1]
    sc = jax.nn.silu(c)
    scc = jax.nn.silu(c_ctx)
    for l in range(DEPTH):
        last = l == DEPTH - 1
        mod = (sc @ w_mod[l] + b_mod[l])[:, None, :]
        mod_c = scc @ w_mod[l] + b_mod[l]
        sh1, s1, g1, sh2, s2, g2 = jnp.split(mod, 6, axis=-1)
        csh1, cs1, cg1, csh2, cs2, cg2 = jnp.split(mod_c, 6, axis=-1)
        y, yc = _token_mixer(_modulate(x, sh1, s1), _modulate(ctx, csh1, cs1), w_in[l],
                             conv_w[l], conv_b[l], lower_bounds[l], hg_norm_g[l],
                             w_pa[l], w_pb[l], w_o[l], ctx_out=not last)
        x = _layer_norm(ALPHA * x + g1 * y, ln1_g[l], ln1_b[l])
        f = _peer_ffn(_modulate(x, sh2, s2), peer_wq[l], peer_keys[l], peer_u[l], peer_v[l])
        x = _layer_norm(ALPHA * x + g2 * f, ln2_g[l], ln2_b[l])
        if not last:
            ctx = _layer_norm(ALPHA * ctx + cg1 * yc, ln1_g[l], ln1_b[l])
            fc = _peer_ffn(_modulate(ctx, csh2, cs2), peer_wq[l], peer_keys[l], peer_u[l], peer_v[l])
            ctx = _layer_norm(ALPHA * ctx + cg2 * fc, ln2_g[l], ln2_b[l])
    return x
```

```python
import functools
import math

import numpy as np
import jax
import jax.numpy as jnp
from jax import lax
from jax.experimental import pallas as pl
from jax.experimental.pallas import tpu as pltpu

F32 = jnp.float32
BF16 = jnp.bfloat16

LANES = 128
SUBLANES = 8
VMEM_LIMIT = 56 * 1024 * 1024

EPS = 1e-6
F_MIN = 1e-30
GRID_W = 64
HG_DK = 128
HG_DV = 128
CHUNK = 64
LEVELS = (32, 16, 8, 4, 2, 1, 0)
N_KEYS = 128
TOPK = 16
PEER_HEADS = 8
G_PITCH = 136


def _cparams(sem):
    return pltpu.CompilerParams(dimension_semantics=sem, vmem_limit_bytes=VMEM_LIMIT)


def _tile(n, pref):
    t = min(n, pref)
    while n % t:
        t //= 2
    return t


def _ln_rows(x):
    mu = jnp.mean(x, axis=-1, keepdims=True)
    xc = x - mu
    var = jnp.mean(xc * xc, axis=-1, keepdims=True)
    return xc * lax.rsqrt(var + EPS)


def _sigmoid(x):
    return 1.0 / (1.0 + jnp.exp(-x))


def _silu(x):
    return x * _sigmoid(x)


def _split_bf16(x):
    hi = x.astype(BF16)
    lo = (x - hi.astype(F32)).astype(BF16)
    return hi, lo


def _mod_kernel(ct_ref, w_ref, b_ref, o_ref, *, n_rows):
    d_model = w_ref.shape[1]
    tn = w_ref.shape[2]

    def body(g, accs):
        r0 = pl.multiple_of(g * SUBLANES, SUBLANES)
        w8 = w_ref[0, pl.ds(r0, SUBLANES), :]
        s8 = _silu(ct_ref[pl.ds(r0, SUBLANES), :])
        return tuple(acc + s8[:, r:r + 1] * w8 for r, acc in enumerate(accs))

    init = tuple(jnp.zeros((SUBLANES, tn), F32) for _ in range(n_rows))
    accs = lax.fori_loop(0, d_model // SUBLANES, body, init)
    bias = b_ref[0]
    o_ref[0] = jnp.zeros((o_ref.shape[1], tn), F32)
    for r, acc in enumerate(accs):
        o_ref[0, r:r + 1, :] = jnp.sum(acc, axis=0, keepdims=True) + bias


def _mod_call(ct, w_mod, b_mod, n_rows):
    depth, d_model, width = w_mod.shape
    tn = 512
    return pl.pallas_call(
        functools.partial(_mod_kernel, n_rows=n_rows),
        grid=(depth, width // tn),
        in_specs=[
            pl.BlockSpec((d_model, LANES), lambda l, j: (0, 0)),
            pl.BlockSpec((1, d_model, tn), lambda l, j: (l, 0, j)),
            pl.BlockSpec((1, 1, tn), lambda l, j: (l, 0, j)),
        ],
        out_specs=pl.BlockSpec((1, 16, tn), lambda l, j: (l, 0, j)),
        out_shape=jax.ShapeDtypeStruct((depth, 16, width), F32),
        compiler_params=_cparams(("parallel", "parallel")),
        name="mod",
    )(ct, w_mod, b_mod.reshape(depth, 1, width))


def _inproj_kernel(x_ref, sh_ref, sc_ref, w_ref, o_ref, h_ref):
    @pl.when(pl.program_id(1) == 0)
    def _():
        y = _ln_rows(x_ref[...])
        h_ref[...] = (y * (1.0 + sc_ref[0]) + sh_ref[0]).astype(BF16)

    o_ref[...] = jnp.dot(h_ref[...], w_ref[...], preferred_element_type=F32)


def _inproj_call(x2d, mods3, w_bf16, rows_per_mod, mod_row0, tm):
    n, d_model = x2d.shape
    tm = _tile(rows_per_mod or n, tm)
    width = w_bf16.shape[1]
    tn = _tile(width, 1024)

    def mod_idx(g):
        if rows_per_mod is None:
            return lambda i, j: (mod_row0 * 6 + g, 0, 0)
        return lambda i, j: ((mod_row0 + (i * tm) // rows_per_mod) * 6 + g, 0, 0)

    return pl.pallas_call(
        _inproj_kernel,
        grid=(n // tm, width // tn),
        in_specs=[
            pl.BlockSpec((tm, d_model), lambda i, j: (i, 0)),
            pl.BlockSpec((1, 1, d_model), mod_idx(0)),
            pl.BlockSpec((1, 1, d_model), mod_idx(1)),
            pl.BlockSpec((d_model, tn), lambda i, j: (0, j)),
        ],
        out_specs=pl.BlockSpec((tm, tn), lambda i, j: (i, j)),
        out_shape=jax.ShapeDtypeStruct((n, width), F32),
        scratch_shapes=[pltpu.VMEM((tm, d_model), BF16)],
        compiler_params=_cparams(("parallel", "arbitrary")),
        name="inproj",
    )(x2d, mods3, mods3, w_bf16)


def _decay_matrices(reverse):
    L = CHUNK
    rank = np.arange(L)[::-1] if reverse else np.arange(L)
    rt = rank[:, None]
    rr = rank[None, :]
    blocks = [(rr <= rt), (rr > rt)]
    for h in LEVELS:
        if h == 0:
            blocks += [np.zeros((L, L), bool), np.zeros((L, L), bool)]
            continue
        upper = (rt & h) != 0
        blocks.append(upper & (rr >= (rt & ~(h - 1))) & (rr <= rt))
        blocks.append((~upper) & (rr > rt) & (rr <= (rt | (h - 1))))
    return np.concatenate(blocks, axis=0).astype(np.float32)


def _level_masks(reverse):
    L = CHUNK
    rank = np.arange(L)[::-1] if reverse else np.arange(L)
    rt = rank[:, None]
    rs = rank[None, :]
    out = []
    for h in LEVELS:
        if h == 0:
            out.append(rt == rs)
        else:
            out.append(((rt ^ rs) // h == 1) & (rt > rs))
    return np.stack(out).astype(np.float32)


def _hgrn_chunk(q, z, v, lb, dmat, masks, state):
    sig = _sigmoid(z)
    f = lb + (1.0 - lb) * sig
    g = jnp.log(jnp.maximum(f, F_MIN))
    k = (1.0 - lb) * (1.0 - sig)
    qs = _silu(q) * (HG_DK ** -0.5)

    g_hi, g_lo = _split_bf16(g)
    e2 = jnp.dot(dmat, jnp.concatenate([g_hi, g_lo], axis=1), preferred_element_type=F32)
    e = e2[:, :LANES] + e2[:, LANES:]
    L = CHUNK
    bcum = e[0:L]
    rem = e[L:2 * L]

    v16 = v.astype(BF16)
    scores = jnp.zeros((L, L), F32)
    for i, h in enumerate(LEVELS):
        if h == 0:
            qt, kt = qs, k
        else:
            qt = qs * jnp.exp(e[(2 + 2 * i) * L:(3 + 2 * i) * L])
            kt = k * jnp.exp(e[(3 + 2 * i) * L:(4 + 2 * i) * L])
        s_h = lax.dot_general(qt.astype(BF16), kt.astype(BF16), (((1,), (1,)), ((), ())),
                              preferred_element_type=F32)
        scores = scores + jnp.where(masks[i] > 0.5, s_h, 0.0)

    o = jnp.dot(scores.astype(BF16), v16, preferred_element_type=F32)
    o = o + jnp.dot((qs * jnp.exp(bcum)).astype(BF16), state.astype(BF16), preferred_element_type=F32)

    ones = jnp.ones((2 * L, HG_DV), BF16)
    tot = lax.dot_general(jnp.concatenate([g_hi, g_lo], axis=0), ones, (((0,), (0,)), ((), ())),
                          preferred_element_type=F32)
    kv = lax.dot_general((k * jnp.exp(rem)).astype(BF16), v16, (((0,), (0,)), ((), ())),
                         preferred_element_type=F32)
    return o, jnp.exp(tot) * state + kv


def _hgrn_kernel(ql, zfl, zbl, vl, ogl, qc, zfc, zbc, vc, ogc, lb_ref, ng_ref, dm_ref, mk_ref,
                 yl_ref, yc_ref, ol_s, oc_s):
    lbf = lb_ref[0:1, :]
    lbb = lb_ref[1:2, :]
    dmf = dm_ref[0]
    dmb = dm_ref[1]
    mkf = mk_ref[0]
    mkb = mk_ref[1]

    def run(q_ref, zf_ref, zb_ref, v_ref, o_s, states):
        n = q_ref.shape[0] // CHUNK

        def body(c, carry):
            sf, sb = carry
            rf = pl.multiple_of(c * CHUNK, CHUNK)
            rb = pl.multiple_of((n - 1 - c) * CHUNK, CHUNK)
            of, sf = _hgrn_chunk(q_ref[pl.ds(rf, CHUNK), :], zf_ref[pl.ds(rf, CHUNK), :],
                                 v_ref[pl.ds(rf, CHUNK), :], lbf, dmf, mkf, sf)
            ob, sb = _hgrn_chunk(q_ref[pl.ds(rb, CHUNK), :], zb_ref[pl.ds(rb, CHUNK), :],
                                 v_ref[pl.ds(rb, CHUNK), :], lbb, dmb, mkb, sb)
            o_s[pl.ds(rf, CHUNK), :] += of
            o_s[pl.ds(rb, CHUNK), :] += ob
            return sf, sb

        o_s[...] = jnp.zeros(o_s.shape, F32)
        return lax.fori_loop(0, n, body, states)

    zero = jnp.zeros((HG_DK, HG_DV), F32)
    states = run(qc, zfc, zbc, vc, oc_s, (zero, zero))
    run(ql, zfl, zbl, vl, ol_s, states)

    def readout(o_s, og_ref, y_ref):
        o = o_s[...]
        o = o * lax.rsqrt(jnp.mean(o * o, axis=-1, keepdims=True) + EPS) * ng_ref[...]
        y_ref[...] = (o * _silu(og_ref[...])).astype(y_ref.dtype)

    readout(ol_s, ogl, yl_ref)
    readout(oc_s, ogc, yc_ref)


def _hgrn_call(p_lat, p_ctx, lb, norm_g, batch, d_model):
    n_lat = p_lat.shape[0]
    n_ctx = p_ctx.shape[0]
    s_len = n_lat // batch
    c_len = n_ctx // batch
    d_half = d_model // 2
    heads = d_half // HG_DK
    cb = d_half // LANES
    q0, zf0, zb0, v0, og0 = 3 * cb, 4 * cb, 5 * cb, 6 * cb, 7 * cb

    def spec(rows, c0):
        return pl.BlockSpec((rows, LANES), lambda b, h, c0=c0: (b, c0 + h))

    dmat = jnp.asarray(np.stack([_decay_matrices(False), _decay_matrices(True)]), BF16)
    masks = jnp.asarray(np.stack([_level_masks(False), _level_masks(True)]), F32)
    full = lambda shape: pl.BlockSpec(shape, lambda b, h: (0,) * len(shape))

    return pl.pallas_call(
        _hgrn_kernel,
        grid=(batch, heads),
        in_specs=[spec(s_len, c0) for c0 in (q0, zf0, zb0, v0, og0)]
        + [spec(c_len, c0) for c0 in (q0, zf0, zb0, v0, og0)]
        + [pl.BlockSpec((2, LANES), lambda b, h: (0, h)),
           full((1, HG_DV)), full(dmat.shape), full(masks.shape)],
        out_specs=[pl.BlockSpec((s_len, LANES), lambda b, h: (b, h)),
                   pl.BlockSpec((c_len, LANES), lambda b, h: (b, h))],
        out_shape=[jax.ShapeDtypeStruct((n_lat, d_half), BF16),
                   jax.ShapeDtypeStruct((n_ctx, d_half), BF16)],
        scratch_shapes=[pltpu.VMEM((s_len, HG_DV), F32), pltpu.VMEM((c_len, HG_DV), F32)],
        compiler_params=_cparams(("parallel", "parallel")),
        name="hgrn",
    )(*([p_lat] * 5), *([p_ctx] * 5), lb, norm_g.reshape(1, HG_DV), dmat, masks)


def _merge_kernel(cb_ref, cc_ref, cv_ref, ga_ref, gb_ref, yb_ref, x_ref, g1_ref, cw_ref, cbias_ref,
                  wpa_ref, wpb_ref, wo_ref, lng_ref, lnb_ref, o_ref, *, row_len, alpha):
    tm = x_ref.shape[0]
    u = cc_ref[...] * cv_ref[...]
    pos = lax.broadcasted_iota(jnp.int32, (tm, 1), 0) % row_len
    prev = jnp.where(pos == 0, 0.0, pltpu.roll(u, 1, axis=0))
    nxt = jnp.where(pos == row_len - 1, 0.0, pltpu.roll(u, tm - 1, axis=0))
    conv = prev * cw_ref[0:1, :] + u * cw_ref[1:2, :] + nxt * cw_ref[2:3, :] + cbias_ref[...]
    ya = (cb_ref[...] * conv).astype(BF16)
    m = (_sigmoid(ga_ref[...]) * jnp.dot(ya, wpa_ref[...], preferred_element_type=F32)
         + _sigmoid(gb_ref[...]) * jnp.dot(yb_ref[...], wpb_ref[...], preferred_element_type=F32))
    y = jnp.dot(m.astype(BF16), wo_ref[...], preferred_element_type=F32)
    r = alpha * x_ref[...] + g1_ref[0] * y
    o_ref[...] = _ln_rows(r) * lng_ref[...] + lnb_ref[...]


def _merge_call(p, yb, x2d, mods3, conv_w, conv_b, wpa, wpb, wo, ln_g, ln_b, rows_per_mod, mod_row0,
                row_len, tm, alpha):
    n, d_model = x2d.shape
    tm = _tile(rows_per_mod or n, tm)
    d_half = d_model // 2

    if rows_per_mod is None:
        g1_idx = lambda i: (mod_row0 * 6 + 2, 0, 0)
    else:
        g1_idx = lambda i: ((mod_row0 + (i * tm) // rows_per_mod) * 6 + 2, 0, 0)
    const = lambda shape: pl.BlockSpec(shape, lambda i: (0,) * len(shape), pipeline_mode=pl.Buffered(1))
    return pl.pallas_call(
        functools.partial(_merge_kernel, row_len=row_len, alpha=alpha),
        grid=(n // tm,),
        in_specs=[
            pl.BlockSpec((tm, d_half), lambda i: (i, 0)),
            pl.BlockSpec((tm, d_half), lambda i: (i, 1)),
            pl.BlockSpec((tm, d_half), lambda i: (i, 2)),
            pl.BlockSpec((tm, d_model), lambda i: (i, 4)),
            pl.BlockSpec((tm, d_model), lambda i: (i, 5)),
            pl.BlockSpec((tm, d_half), lambda i: (i, 0)),
            pl.BlockSpec((tm, d_model), lambda i: (i, 0)),
            pl.BlockSpec((1, 1, d_model), g1_idx),
            const((3, d_half)), const((1, d_half)),
            const((d_half, d_model)), const((d_half, d_model)), const((d_model, d_model)),
            const((1, d_model)), const((1, d_model)),
        ],
        out_specs=pl.BlockSpec((tm, d_model), lambda i: (i, 0)),
        out_shape=jax.ShapeDtypeStruct((n, d_model), F32),
        compiler_params=_cparams(("parallel",)),
        name="merge",
    )(p, p, p, p, p, yb, x2d, mods3, conv_w, conv_b.reshape(1, d_half), wpa, wpb, wo,
      ln_g.reshape(1, d_model), ln_b.reshape(1, d_model))


def _topk_rows(s, k, payload=None):
    rows = s.shape[0]
    riota = lax.broadcasted_iota(jnp.int32, s.shape, 0)
    vals, picks = [], []
    for _ in range(k):
        m = jnp.max(s, axis=0, keepdims=True)
        am = jnp.min(jnp.where(s == m, riota, rows), axis=0, keepdims=True)
        sel = riota == am
        vals.append(m)
        if payload is None:
            picks.append(am)
        else:
            picks.append(jnp.max(jnp.where(sel, payload, -1), axis=0, keepdims=True))
        s = jnp.where(sel, -jnp.inf, s)
    return jnp.concatenate(vals, axis=0), jnp.concatenate(picks, axis=0)


def _route_kernel(x_ref, sh_ref, sc_ref, wq_ref, keys_ref, h_ref, ii_ref, jj_ref, gg_ref):
    hmod = _ln_rows(x_ref[...]) * (1.0 + sc_ref[0]) + sh_ref[0]
    h16 = hmod.astype(BF16)
    h_ref[...] = h16
    qp = jnp.dot(h16, wq_ref[...], preferred_element_type=F32).astype(BF16)
    i_rows, j_rows, g_rows = [], [], []
    for head in range(PEER_HEADS):
        sv, si = [], []
        for half in range(2):
            c = (head * 2 + half) * LANES
            s = lax.dot_general(keys_ref[head * 2 + half], qp[:, c:c + LANES], (((1,), (1,)), ((), ())),
                                preferred_element_type=F32)
            v_, i_ = _topk_rows(s, TOPK)
            sv.append(v_)
            si.append(i_)
        cand = jnp.concatenate([sv[0][a:a + 1, :] + sv[1] for a in range(TOPK)], axis=0)
        cidx = jnp.concatenate([si[0][a:a + 1, :] * N_KEYS + si[1] for a in range(TOPK)], axis=0)
        tv, te = _topk_rows(cand, TOPK, payload=cidx)
        ex = jnp.exp(tv - tv[0:1, :])
        g_rows.append(ex / jnp.sum(ex, axis=0, keepdims=True))
        i_rows.append(lax.shift_right_logical(te, 7))
        j_rows.append(te & (N_KEYS - 1))
    ii_ref[...] = jnp.concatenate(i_rows, axis=0).T
    jj_ref[...] = jnp.concatenate(j_rows, axis=0).T
    gg_ref[...] = jnp.concatenate(g_rows, axis=0).T


def _route_call(x2d, mods3, wq, keys, rows_per_mod, mod_row0, tm):
    n, d_model = x2d.shape
    tm = _tile(rows_per_mod or n, tm)
    slots = PEER_HEADS * TOPK

    def mod_idx(g):
        if rows_per_mod is None:
            return lambda i: (mod_row0 * 6 + g, 0, 0)
        return lambda i: ((mod_row0 + (i * tm) // rows_per_mod) * 6 + g, 0, 0)

    const = lambda shape: pl.BlockSpec(shape, lambda i: (0,) * len(shape), pipeline_mode=pl.Buffered(1))
    return pl.pallas_call(
        _route_kernel,
        grid=(n // tm,),
        in_specs=[
            pl.BlockSpec((tm, d_model), lambda i: (i, 0)),
            pl.BlockSpec((1, 1, d_model), mod_idx(3)),
            pl.BlockSpec((1, 1, d_model), mod_idx(4)),
            const(wq.shape), const(keys.shape),
        ],
        out_specs=[pl.BlockSpec((tm, d_model), lambda i: (i, 0))]
        + [pl.BlockSpec((tm, slots), lambda i: (i, 0))] * 3,
        out_shape=[jax.ShapeDtypeStruct((n, d_model), BF16),
                   jax.ShapeDtypeStruct((n, slots), jnp.int32),
                   jax.ShapeDtypeStruct((n, slots), jnp.int32),
                   jax.ShapeDtypeStruct((n, slots), F32)],
        compiler_params=_cparams(("parallel",)),
        name="route",
    )(x2d, mods3, mods3, wq, keys)


def _gelu_exact(x):
    return 0.5 * x * (1.0 + lax.erf(x * (2.0 ** -0.5)))


def _dense_kernel(h_ref, ii_ref, jj_ref, gg_ref, ut_ref, v_ref, x_ref, g2_ref, lng_ref, lnb_ref, o_ref,
                  gate_s, acc_s, *, alpha):
    tm = h_ref.shape[0]
    eb = pl.program_id(1)
    keys_per_block = ut_ref.shape[1] // N_KEYS

    @pl.when(eb == 0)
    def _():
        acc_s[...] = jnp.zeros(acc_s.shape, F32)
        sub = lax.broadcasted_iota(jnp.int32, (N_KEYS, N_KEYS), 0)

        def token(n, carry):
            ii = ii_ref[pl.ds(n, 1), :]
            jj = jj_ref[pl.ds(n, 1), :]
            gg = gg_ref[pl.ds(n, 1), :]
            g_hi = gg.astype(BF16).astype(F32)
            g_lo = gg - g_hi
            hit_i = ii == sub
            a_t = jnp.concatenate([jnp.where(hit_i, g_hi, 0.0), jnp.where(hit_i, g_lo, 0.0)],
                                  axis=1).astype(BF16)
            b1 = jnp.where(jj == sub, 1.0, 0.0).astype(BF16)
            b_t = jnp.concatenate([b1, b1], axis=1)
            gate = lax.dot_general(a_t, b_t, (((1,), (1,)), ((), ())), preferred_element_type=F32)
            gate_s[pl.ds(pl.multiple_of(n * G_PITCH, SUBLANES), N_KEYS), :] = gate
            return carry

        lax.fori_loop(0, tm, token, 0)

    act = _gelu_exact(jnp.dot(h_ref[...], ut_ref[...], preferred_element_type=F32))
    parts = []
    for r in range(keys_per_block):
        gi = gate_s[pl.ds(eb * keys_per_block + r, tm, stride=G_PITCH), :]
        parts.append((gi * act[:, r * N_KEYS:(r + 1) * N_KEYS]).astype(BF16))
    hg = jnp.concatenate(parts, axis=1)
    acc_s[...] += jnp.dot(hg, v_ref[...], preferred_element_type=F32)

    @pl.when(eb == pl.num_programs(1) - 1)
    def _():
        r = alpha * x_ref[...] + g2_ref[0] * acc_s[...]
        o_ref[...] = _ln_rows(r) * lng_ref[...] + lnb_ref[...]


def _dense_call(h16, ii, jj, gg, ut, v16, x2d, mods3, ln_g, ln_b, rows_per_mod, mod_row0, tm, alpha):
    n, d_model = x2d.shape
    tm = _tile(rows_per_mod or n, tm)
    n_exp = v16.shape[0]
    slots = ii.shape[1]
    te = 512

    if rows_per_mod is None:
        g2_idx = lambda i, e: (mod_row0 * 6 + 5, 0, 0)
    else:
        g2_idx = lambda i, e: ((mod_row0 + (i * tm) // rows_per_mod) * 6 + 5, 0, 0)
    tok = lambda cols: pl.BlockSpec((tm, cols), lambda i, e: (i, 0))
    const = lambda shape: pl.BlockSpec(shape, lambda i, e: (0,) * len(shape))
    return pl.pallas_call(
        functools.partial(_dense_kernel, alpha=alpha),
        grid=(n // tm, n_exp // te),
        in_specs=[
            tok(d_model), tok(slots), tok(slots), tok(slots),
            pl.BlockSpec((d_model, te), lambda i, e: (0, e)),
            pl.BlockSpec((te, d_model), lambda i, e: (e, 0)),
            tok(d_model),
            pl.BlockSpec((1, 1, d_model), g2_idx),
            const((1, d_model)), const((1, d_model)),
        ],
        out_specs=tok(d_model),
        out_shape=jax.ShapeDtypeStruct((n, d_model), F32),
        scratch_shapes=[pltpu.VMEM((tm * G_PITCH, LANES), F32), pltpu.VMEM((tm, d_model), F32)],
        compiler_params=_cparams(("parallel", "arbitrary")),
        name="dense",
    )(h16, ii, jj, gg, ut, v16, x2d, mods3, ln_g.reshape(1, d_model), ln_b.reshape(1, d_model))


def kernel(x, c, ctx, c_ctx, w_mod, b_mod, w_in, conv_w, conv_b, lb_raw, hg_norm_g, w_pa, w_pb, w_o,
           ln1_g, ln1_b, peer_wq, peer_keys, peer_u, peer_v, ln2_g, ln2_b):
    batch, s_len, d_model = x.shape
    c_len = ctx.shape[1]
    depth = w_mod.shape[0]
    alpha = (2.0 * depth) ** 0.25
    n_lat = batch * s_len
    n_ctx = batch * c_len

    p = jax.nn.softmax(lb_raw.astype(F32), axis=0)
    lower = jnp.cumsum(p, axis=0) - p[:1]

    cond = jnp.concatenate([c, c_ctx[None, :]], axis=0)
    ct = jnp.zeros((d_model, LANES), F32).at[:, :batch + 1].set(cond.T)
    mods = _mod_call(ct, w_mod, b_mod, batch + 1)

    xl = x.reshape(n_lat, d_model)
    xc = ctx.reshape(n_ctx, d_model)
    for l in range(depth):
        last = l == depth - 1
        mods3 = mods[l].reshape(16 * 6, 1, d_model)
        w_in16 = w_in[l].astype(BF16)
        wpa, wpb, wo = w_pa[l].astype(BF16), w_pb[l].astype(BF16), w_o[l].astype(BF16)
        wq = peer_wq[l].astype(BF16)
        keys = peer_keys[l].reshape(PEER_HEADS * 2, N_KEYS, -1).astype(BF16)
        ut = peer_u[l].astype(BF16).T
        v16 = peer_v[l].astype(BF16)

        p_lat = _inproj_call(xl, mods3, w_in16, s_len, 0, 1024)
        p_ctx = _inproj_call(xc, mods3, w_in16, None, batch, 1024)
        yb_lat, yb_ctx = _hgrn_call(p_lat, p_ctx, lower[l], hg_norm_g[l], batch, d_model)

        streams = [(xl, p_lat, yb_lat, s_len, 0, GRID_W)]
        if not last:
            streams.append((xc, p_ctx, yb_ctx, None, batch, c_len))
        outs = []
        for xs, ps, ybs, rpm, row0, row_len in streams:
            x1 = _merge_call(ps, ybs, xs, mods3, conv_w[l], conv_b[l], wpa, wpb, wo, ln1_g[l], ln1_b[l],
                             rpm, row0, row_len, 256, alpha)
            h16, ii, jj, gg = _route_call(x1, mods3, wq, keys, rpm, row0, 256)
            outs.append(_dense_call(h16, ii, jj, gg, ut, v16, x1, mods3, ln2_g[l], ln2_b[l],
                                    rpm, row0, 256, alpha))
        xl = outs[0]
        if not last:
            xc = outs[1]
    return xl.reshape(batch, s_len, d_model)
```

```python
import functools
import math

import numpy as np
import jax
import jax.numpy as jnp
from jax import lax
from jax.experimental import pallas as pl
from jax.experimental.pallas import tpu as pltpu

F32 = jnp.float32
BF16 = jnp.bfloat16

LANES = 128
SUBLANES = 8
VMEM_LIMIT = 56 * 1024 * 1024

EPS = 1e-6
F_MIN = 1e-30
GRID_W = 64
HG_DK = 128
HG_DV = 128
CHUNK = 64
LEVELS = (32, 16, 8, 4, 2, 1, 0)
MATMUL_LEVELS = (4, 2)
INTRA_CHUNKS = 4
N_KEYS = 128
TOPK = 16
PEER_HEADS = 8
G_PITCH = 136
DENSE_TE = 1024


def _cparams(sem):
    return pltpu.CompilerParams(dimension_semantics=sem, vmem_limit_bytes=VMEM_LIMIT)


def _tile(n, pref):
    t = min(n, pref)
    while n % t:
        t //= 2
    return t


def _ln_rows(x):
    mu = jnp.mean(x, axis=-1, keepdims=True)
    xc = x - mu
    var = jnp.mean(xc * xc, axis=-1, keepdims=True)
    return xc * lax.rsqrt(var + EPS)


def _sigmoid(x):
    return 1.0 / (1.0 + jnp.exp(-x))


def _silu(x):
    return x * _sigmoid(x)


def _split_bf16(x):
    hi = x.astype(BF16)
    lo = (x - hi.astype(F32)).astype(BF16)
    return hi, lo


def _mod_kernel(cb_ref, w_ref, b_ref, o_ref, *, n_rows):
    d_model = w_ref.shape[1]
    tn = w_ref.shape[2]

    def body(g, accs):
        r0 = pl.multiple_of(g * SUBLANES, SUBLANES)
        w8 = w_ref[0, pl.ds(r0, SUBLANES), :]
        out = []
        for r, acc in enumerate(accs):
            s8 = _silu(cb_ref[pl.ds(r0, SUBLANES), r * LANES:(r + 1) * LANES])
            out.append(acc + jnp.concatenate([s8] * (tn // LANES), axis=1) * w8)
        return tuple(out)

    init = tuple(jnp.zeros((SUBLANES, tn), F32) for _ in range(n_rows))
    accs = lax.fori_loop(0, d_model // SUBLANES, body, init, unroll=2)
    bias = b_ref[0]
    o_ref[0] = jnp.zeros((o_ref.shape[1], tn), F32)
    for r, acc in enumerate(accs):
        o_ref[0, r:r + 1, :] = jnp.sum(acc, axis=0, keepdims=True) + bias


def _mod_call(cb, w_mod, b_mod, n_rows):
    depth, d_model, width = w_mod.shape
    tn = 512
    return pl.pallas_call(
        functools.partial(_mod_kernel, n_rows=n_rows),
        grid=(depth, width // tn),
        in_specs=[
            pl.BlockSpec((d_model, n_rows * LANES), lambda l, j: (0, 0), pipeline_mode=pl.Buffered(1)),
            pl.BlockSpec((1, d_model, tn), lambda l, j: (l, 0, j)),
            pl.BlockSpec((1, 1, tn), lambda l, j: (l, 0, j)),
        ],
        out_specs=pl.BlockSpec((1, 16, tn), lambda l, j: (l, 0, j)),
        out_shape=jax.ShapeDtypeStruct((depth, 16, width), F32),
        compiler_params=_cparams(("parallel", "parallel")),
        name="mod",
    )(cb, w_mod, b_mod.reshape(depth, 1, width))


def _inproj_kernel(x_ref, sh_ref, sc_ref, w_ref, o_ref, h_ref):
    @pl.when(pl.program_id(1) == 0)
    def _():
        y = _ln_rows(x_ref[...])
        h_ref[...] = (y * (1.0 + sc_ref[0]) + sh_ref[0]).astype(BF16)

    o_ref[...] = jnp.dot(h_ref[...], w_ref[...], preferred_element_type=F32)


def _inproj_call(x2d, mods3, w_bf16, rows_per_mod, mod_row0, tm):
    n, d_model = x2d.shape
    tm = _tile(rows_per_mod or n, tm)
    width = w_bf16.shape[1]
    tn = _tile(width, 1024)

    def mod_idx(g):
        if rows_per_mod is None:
            return lambda i, j: (mod_row0 * 6 + g, 0, 0)
        return lambda i, j: ((mod_row0 + (i * tm) // rows_per_mod) * 6 + g, 0, 0)

    return pl.pallas_call(
        _inproj_kernel,
        grid=(n // tm, width // tn),
        in_specs=[
            pl.BlockSpec((tm, d_model), lambda i, j: (i, 0)),
            pl.BlockSpec((1, 1, d_model), mod_idx(0)),
            pl.BlockSpec((1, 1, d_model), mod_idx(1)),
            pl.BlockSpec((d_model, tn), lambda i, j: (0, j)),
        ],
        out_specs=pl.BlockSpec((tm, tn), lambda i, j: (i, j)),
        out_shape=jax.ShapeDtypeStruct((n, width), F32),
        scratch_shapes=[pltpu.VMEM((tm, d_model), BF16)],
        compiler_params=_cparams(("parallel", "arbitrary")),
        name="inproj",
    )(x2d, mods3, mods3, w_bf16)


def _decay_matrices(reverse):
    L = CHUNK
    rank = np.arange(L)[::-1] if reverse else np.arange(L)
    rt = rank[:, None]
    rr = rank[None, :]
    blocks = [(rr <= rt)]
    for h in MATMUL_LEVELS:
        upper = (rt & h) != 0
        blocks.append(upper & (rr >= (rt & ~(h - 1))) & (rr <= rt))
        blocks.append((~upper) & (rr > rt) & (rr <= (rt | (h - 1))))
    return np.concatenate(blocks, axis=0).astype(np.float32)


def _half_block_exponents(bcum, h, reverse):
    eq, ek = [], []
    zero = jnp.zeros((h, LANES), F32)
    for b in range(CHUNK // h):
        rows = bcum[b * h:(b + 1) * h]
        first_of_pair = b % 2 == 0
        if not reverse:
            if first_of_pair:
                eq.append(zero)
                ek.append(bcum[(b + 1) * h - 1:(b + 1) * h] - rows)
            else:
                eq.append(rows - bcum[b * h - 1:b * h])
                ek.append(zero)
        else:
            if first_of_pair:
                eq.append(rows - bcum[(b + 1) * h:(b + 1) * h + 1])
                ek.append(zero)
            else:
                eq.append(zero)
                ek.append(bcum[b * h:b * h + 1] - rows)
    return jnp.concatenate(eq, axis=0), jnp.concatenate(ek, axis=0)


def _level_masks(reverse):
    L = CHUNK
    rank = np.arange(L)[::-1] if reverse else np.arange(L)
    rt = rank[:, None]
    rs = rank[None, :]
    out = []
    for h in LEVELS:
        if h == 0:
            out.append(rt == rs)
        else:
            out.append(((rt ^ rs) // h == 1) & (rt > rs))
    return np.stack(out).astype(np.float32)


def _hgrn_intra(chains):
    L = CHUNK
    gs, ks, es, bcums, blasts = [], [], [], [], []
    for qs, z, v16, lb, dmat, masks, reverse in chains:
        sig = _sigmoid(z)
        f = lb + (1.0 - lb) * sig
        g = jnp.log(jnp.maximum(f, F_MIN))
        g_hi, g_lo = _split_bf16(g)
        e2 = jnp.dot(dmat, jnp.concatenate([g_hi, g_lo], axis=1), preferred_element_type=F32)
        e = e2[:, :LANES] + e2[:, LANES:]
        gs.append(g)
        ks.append((1.0 - lb) * (1.0 - sig))
        es.append(e)
        bcums.append(e[0:L])
        blasts.append(e[0:1] if reverse else e[L - 1:L])

    scores = [jnp.zeros((L, L), F32) for _ in chains]
    for i, h in enumerate(LEVELS):
        for c, (qs, z, v16, lb, dmat, masks, reverse) in enumerate(chains):
            k = ks[c]
            if h == 0:
                qt, kt = qs, k
            elif h == 1:
                qt, kt = qs * jnp.exp(gs[c]), k
            elif h in MATMUL_LEVELS:
                m = 1 + 2 * MATMUL_LEVELS.index(h)
                qt = qs * jnp.exp(es[c][m * L:(m + 1) * L])
                kt = k * jnp.exp(es[c][(m + 1) * L:(m + 2) * L])
            else:
                eq, ek = _half_block_exponents(bcums[c], h, reverse)
                qt = qs * jnp.exp(eq)
                kt = k * jnp.exp(ek)
            s_h = lax.dot_general(qt.astype(BF16), kt.astype(BF16), (((1,), (1,)), ((), ())),
                                  preferred_element_type=F32)
            scores[c] = scores[c] + masks[i] * s_h

    outs = []
    for c, (qs, z, v16, lb, dmat, masks, reverse) in enumerate(chains):
        o = jnp.dot(scores[c].astype(BF16), v16, preferred_element_type=F32)
        qhat = (qs * jnp.exp(bcums[c])).astype(BF16)
        kvt = lax.dot_general(v16, (ks[c] * jnp.exp(blasts[c] - bcums[c])).astype(BF16),
                              (((0,), (0,)), ((), ())), preferred_element_type=F32)
        outs.append((o, qhat, kvt, blasts[c]))
    return outs


def _hgrn_kernel(ql, zfl, zbl, vl, ogl, qc, zfc, zbc, vc, ogc, lb_ref, ng_ref, dm_ref, mk_ref,
                 yl_ref, yc_ref, ol_s, oc_s, qh_s, kv_s, bl_s, st_s):
    lbf = lb_ref[0:1, :]
    lbb = lb_ref[1:2, :]

    def run(q_ref, zf_ref, zb_ref, v_ref, o_s, states):
        n = q_ref.shape[0] // CHUNK
        per_it = math.gcd(n, INTRA_CHUNKS)

        def intra(it, carry):
            chains = []
            for u in range(per_it):
                c = it * per_it + u
                r = pl.multiple_of(c * CHUNK, CHUNK)
                qs = _silu(q_ref[pl.ds(r, CHUNK), :]) * (HG_DK ** -0.5)
                v16 = v_ref[pl.ds(r, CHUNK), :].astype(BF16)
                chains.append((qs, zf_ref[pl.ds(r, CHUNK), :], v16, lbf, dm_ref[0], mk_ref[0], False))
                chains.append((qs, zb_ref[pl.ds(r, CHUNK), :], v16, lbb, dm_ref[1], mk_ref[1], True))
            res = _hgrn_intra(chains)
            for u in range(per_it):
                c = it * per_it + u
                r = pl.multiple_of(c * CHUNK, CHUNK)
                (of, qf, kvf, blf), (ob, qb, kvb, blb) = res[2 * u], res[2 * u + 1]
                o_s[pl.ds(r, CHUNK), :] = of + ob
                qh_s[pl.ds(r, CHUNK), 0:HG_DK] = qf
                qh_s[pl.ds(r, CHUNK), HG_DK:2 * HG_DK] = qb
                kv_s[0, c] = kvf
                kv_s[1, c] = kvb
                bl_s[0, c] = jnp.broadcast_to(blf, (SUBLANES, HG_DK))
                bl_s[1, c] = jnp.broadcast_to(blb, (SUBLANES, HG_DK))
            return carry

        lax.fori_loop(0, n // per_it, intra, 0)

        def recur(c, carry):
            sf, sb = carry
            cb = n - 1 - c
            st_s[c, :, 0:HG_DK] = sf.astype(BF16)
            st_s[cb, :, HG_DK:2 * HG_DK] = sb.astype(BF16)
            sf = sf * jnp.exp(bl_s[0, c][0:1, :]) + kv_s[0, c]
            sb = sb * jnp.exp(bl_s[1, cb][0:1, :]) + kv_s[1, cb]
            return sf, sb

        states = lax.fori_loop(0, n, recur, states)

        def inter(c, carry):
            r = pl.multiple_of(c * CHUNK, CHUNK)
            o_s[pl.ds(r, CHUNK), :] += lax.dot_general(
                qh_s[pl.ds(r, CHUNK), :], st_s[c], (((1,), (1,)), ((), ())), preferred_element_type=F32)
            return carry

        lax.fori_loop(0, n, inter, 0, unroll=4)
        return states

    zero = jnp.zeros((HG_DV, HG_DK), F32)
    states = run(qc, zfc, zbc, vc, oc_s, (zero, zero))
    run(ql, zfl, zbl, vl, ol_s, states)

    def readout(o_s, og_ref, y_ref):
        o = o_s[...]
        o = o * lax.rsqrt(jnp.mean(o * o, axis=-1, keepdims=True) + EPS) * ng_ref[...]
        y_ref[...] = (o * _silu(og_ref[...])).astype(y_ref.dtype)

    readout(ol_s, ogl, yl_ref)
    readout(oc_s, ogc, yc_ref)


def _hgrn_call(p_lat, p_ctx, lb, norm_g, batch, d_model):
    n_lat = p_lat.shape[0]
    n_ctx = p_ctx.shape[0]
    s_len = n_lat // batch
    c_len = n_ctx // batch
    d_half = d_model // 2
    heads = d_half // HG_DK
    cb = d_half // LANES
    q0, zf0, zb0, v0, og0 = 3 * cb, 4 * cb, 5 * cb, 6 * cb, 7 * cb

    def spec(rows, c0):
        return pl.BlockSpec((rows, LANES), lambda b, h, c0=c0: (b, c0 + h))

    dmat = jnp.asarray(np.stack([_decay_matrices(False), _decay_matrices(True)]), BF16)
    masks = jnp.asarray(np.stack([_level_masks(False), _level_masks(True)]), F32)
    full = lambda shape: pl.BlockSpec(shape, lambda b, h: (0,) * len(shape))

    return pl.pallas_call(
        _hgrn_kernel,
        grid=(batch, heads),
        in_specs=[spec(s_len, c0) for c0 in (q0, zf0, zb0, v0, og0)]
        + [spec(c_len, c0) for c0 in (q0, zf0, zb0, v0, og0)]
        + [pl.BlockSpec((2, LANES), lambda b, h: (0, h)),
           full((1, HG_DV)), full(dmat.shape), full(masks.shape)],
        out_specs=[pl.BlockSpec((s_len, LANES), lambda b, h: (b, h)),
                   pl.BlockSpec((c_len, LANES), lambda b, h: (b, h))],
        out_shape=[jax.ShapeDtypeStruct((n_lat, d_half), BF16),
                   jax.ShapeDtypeStruct((n_ctx, d_half), BF16)],
        scratch_shapes=[pltpu.VMEM((s_len, HG_DV), F32), pltpu.VMEM((c_len, HG_DV), F32),
                        pltpu.VMEM((s_len, 2 * HG_DK), BF16),
                        pltpu.VMEM((2, s_len // CHUNK, HG_DV, HG_DK), F32),
                        pltpu.VMEM((2, s_len // CHUNK, SUBLANES, HG_DK), F32),
                        pltpu.VMEM((s_len // CHUNK, HG_DV, 2 * HG_DK), BF16)],
        compiler_params=_cparams(("parallel", "parallel")),
        name="hgrn",
    )(*([p_lat] * 5), *([p_ctx] * 5), lb, norm_g.reshape(1, HG_DV), dmat, masks)


def _merge_kernel(cb_ref, cc_ref, cv_ref, ga_ref, gb_ref, yb_ref, x_ref, g1_ref, cw_ref, cbias_ref,
                  wpa_ref, wpb_ref, wo_ref, lng_ref, lnb_ref, o_ref, *, row_len, alpha):
    tm = x_ref.shape[0]
    u = cc_ref[...] * cv_ref[...]
    pos = lax.broadcasted_iota(jnp.int32, (tm, 1), 0) % row_len
    prev = jnp.where(pos == 0, 0.0, pltpu.roll(u, 1, axis=0))
    nxt = jnp.where(pos == row_len - 1, 0.0, pltpu.roll(u, tm - 1, axis=0))
    conv = prev * cw_ref[0:1, :] + u * cw_ref[1:2, :] + nxt * cw_ref[2:3, :] + cbias_ref[...]
    ya = (cb_ref[...] * conv).astype(BF16)
    m = (_sigmoid(ga_ref[...]) * jnp.dot(ya, wpa_ref[...], preferred_element_type=F32)
         + _sigmoid(gb_ref[...]) * jnp.dot(yb_ref[...], wpb_ref[...], preferred_element_type=F32))
    y = jnp.dot(m.astype(BF16), wo_ref[...], preferred_element_type=F32)
    r = alpha * x_ref[...] + g1_ref[0] * y
    o_ref[...] = _ln_rows(r) * lng_ref[...] + lnb_ref[...]


def _merge_call(p, yb, x2d, mods3, conv_w, conv_b, wpa, wpb, wo, ln_g, ln_b, rows_per_mod, mod_row0,
                row_len, tm, alpha):
    n, d_model = x2d.shape
    tm = _tile(rows_per_mod or n, tm)
    d_half = d_model // 2

    if rows_per_mod is None:
        g1_idx = lambda i: (mod_row0 * 6 + 2, 0, 0)
    else:
        g1_idx = lambda i: ((mod_row0 + (i * tm) // rows_per_mod) * 6 + 2, 0, 0)
    const = lambda shape: pl.BlockSpec(shape, lambda i: (0,) * len(shape), pipeline_mode=pl.Buffered(1))
    return pl.pallas_call(
        functools.partial(_merge_kernel, row_len=row_len, alpha=alpha),
        grid=(n // tm,),
        in_specs=[
            pl.BlockSpec((tm, d_half), lambda i: (i, 0)),
            pl.BlockSpec((tm, d_half), lambda i: (i, 1)),
            pl.BlockSpec((tm, d_half), lambda i: (i, 2)),
            pl.BlockSpec((tm, d_model), lambda i: (i, 4)),
            pl.BlockSpec((tm, d_model), lambda i: (i, 5)),
            pl.BlockSpec((tm, d_half), lambda i: (i, 0)),
            pl.BlockSpec((tm, d_model), lambda i: (i, 0)),
            pl.BlockSpec((1, 1, d_model), g1_idx),
            const((3, d_half)), const((1, d_half)),
            const((d_half, d_model)), const((d_half, d_model)), const((d_model, d_model)),
            const((1, d_model)), const((1, d_model)),
        ],
        out_specs=pl.BlockSpec((tm, d_model), lambda i: (i, 0)),
        out_shape=jax.ShapeDtypeStruct((n, d_model), F32),
        compiler_params=_cparams(("parallel",)),
        name="merge",
    )(p, p, p, p, p, yb, x2d, mods3, conv_w, conv_b.reshape(1, d_half), wpa, wpb, wo,
      ln_g.reshape(1, d_model), ln_b.reshape(1, d_model))


def _topk_rows(s, k, payload=None):
    rows = s.shape[0]
    riota = lax.broadcasted_iota(jnp.int32, s.shape, 0)
    vals, picks = [], []
    for _ in range(k):
        m = jnp.max(s, axis=0, keepdims=True)
        am = jnp.min(jnp.where(s == m, riota, rows), axis=0, keepdims=True)
        sel = riota == am
        vals.append(m)
        if payload is None:
            picks.append(am)
        else:
            picks.append(jnp.max(jnp.where(sel, payload, -1), axis=0, keepdims=True))
        s = jnp.where(sel, -jnp.inf, s)
    return jnp.concatenate(vals, axis=0), jnp.concatenate(picks, axis=0)


def _route_kernel(x_ref, sh_ref, sc_ref, wq_ref, keys_ref, h_ref, ii_ref, jj_ref, gg_ref):
    hmod = _ln_rows(x_ref[...]) * (1.0 + sc_ref[0]) + sh_ref[0]
    h16 = hmod.astype(BF16)
    h_ref[...] = h16
    qp = jnp.dot(h16, wq_ref[...], preferred_element_type=F32).astype(BF16)
    i_rows, j_rows, g_rows = [], [], []
    for head in range(PEER_HEADS):
        sv, si = [], []
        for half in range(2):
            c = (head * 2 + half) * LANES
            s = lax.dot_general(keys_ref[head * 2 + half], qp[:, c:c + LANES], (((1,), (1,)), ((), ())),
                                preferred_element_type=F32)
            v_, i_ = _topk_rows(s, TOPK)
            sv.append(v_)
            si.append(i_)
        cand = jnp.concatenate([sv[0][a:a + 1, :] + sv[1] for a in range(TOPK)], axis=0)
        cidx = jnp.concatenate([si[0][a:a + 1, :] * N_KEYS + si[1] for a in range(TOPK)], axis=0)
        tv, te = _topk_rows(cand, TOPK, payload=cidx)
        ex = jnp.exp(tv - tv[0:1, :])
        g_rows.append(ex / jnp.sum(ex, axis=0, keepdims=True))
        i_rows.append(lax.shift_right_logical(te, 7))
        j_rows.append(te & (N_KEYS - 1))
    ii_ref[...] = jnp.concatenate(i_rows, axis=0).T
    jj_ref[...] = jnp.concatenate(j_rows, axis=0).T
    gg_ref[...] = jnp.concatenate(g_rows, axis=0).T


def _route_call(x2d, mods3, wq, keys, rows_per_mod, mod_row0, tm):
    n, d_model = x2d.shape
    tm = _tile(rows_per_mod or n, tm)
    slots = PEER_HEADS * TOPK

    def mod_idx(g):
        if rows_per_mod is None:
            return lambda i: (mod_row0 * 6 + g, 0, 0)
        return lambda i: ((mod_row0 + (i * tm) // rows_per_mod) * 6 + g, 0, 0)

    const = lambda shape: pl.BlockSpec(shape, lambda i: (0,) * len(shape), pipeline_mode=pl.Buffered(1))
    return pl.pallas_call(
        _route_kernel,
        grid=(n // tm,),
        in_specs=[
            pl.BlockSpec((tm, d_model), lambda i: (i, 0)),
            pl.BlockSpec((1, 1, d_model), mod_idx(3)),
            pl.BlockSpec((1, 1, d_model), mod_idx(4)),
            const(wq.shape), const(keys.shape),
        ],
        out_specs=[pl.BlockSpec((tm, d_model), lambda i: (i, 0))]
        + [pl.BlockSpec((tm, slots), lambda i: (i, 0))] * 3,
        out_shape=[jax.ShapeDtypeStruct((n, d_model), BF16),
                   jax.ShapeDtypeStruct((n, slots), jnp.int32),
                   jax.ShapeDtypeStruct((n, slots), jnp.int32),
                   jax.ShapeDtypeStruct((n, slots), F32)],
        compiler_params=_cparams(("parallel",)),
        name="route",
    )(x2d, mods3, mods3, wq, keys)


def _gelu_exact(x):
    return 0.5 * x * (1.0 + lax.erf(x * (2.0 ** -0.5)))


def _gates_kernel(ii_ref, jj_ref, gg_ref, o_ref, gate_s):
    tm = ii_ref.shape[0]
    sub = lax.broadcasted_iota(jnp.int32, (N_KEYS, N_KEYS), 0)

    def token(n, carry):
        ii = ii_ref[pl.ds(n, 1), :]
        jj = jj_ref[pl.ds(n, 1), :]
        gg = gg_ref[pl.ds(n, 1), :]
        a_t = jnp.where(ii == sub, gg, 0.0).astype(BF16)
        b_t = jnp.where(jj == sub, 1.0, 0.0).astype(BF16)
        gate = lax.dot_general(a_t, b_t, (((1,), (1,)), ((), ())), preferred_element_type=F32)
        gate_s[pl.ds(pl.multiple_of(n * G_PITCH, SUBLANES), N_KEYS), :] = gate
        return carry

    lax.fori_loop(0, tm, token, 0, unroll=16)
    for i in range(N_KEYS):
        o_ref[:, i * N_KEYS:(i + 1) * N_KEYS] = gate_s[pl.ds(i, tm, stride=G_PITCH), :].astype(o_ref.dtype)


def _gates_call(ii, jj, gg, tm):
    n, slots = ii.shape
    tm = _tile(n, tm)
    tok = pl.BlockSpec((tm, slots), lambda i: (i, 0))
    return pl.pallas_call(
        _gates_kernel,
        grid=(n // tm,),
        in_specs=[tok, tok, tok],
        out_specs=pl.BlockSpec((tm, N_KEYS * N_KEYS), lambda i: (i, 0)),
        out_shape=jax.ShapeDtypeStruct((n, N_KEYS * N_KEYS), BF16),
        scratch_shapes=[pltpu.VMEM((tm * G_PITCH, LANES), F32)],
        compiler_params=_cparams(("parallel",)),
        name="gates",
    )(ii, jj, gg)


def _dense_kernel(h_ref, gate_ref, ut_ref, v_ref, x_ref, g2_ref, lng_ref, lnb_ref, o_ref, acc_s, *, alpha):
    eb = pl.program_id(1)

    @pl.when(eb == 0)
    def _():
        acc_s[...] = jnp.zeros(acc_s.shape, F32)

    act = _gelu_exact(jnp.dot(h_ref[...], ut_ref[0], preferred_element_type=F32))
    hg = (gate_ref[...].astype(F32) * act).astype(BF16)
    acc_s[...] += jnp.dot(hg, v_ref[...], preferred_element_type=F32)

    @pl.when(eb == pl.num_programs(1) - 1)
    def _():
        r = alpha * x_ref[...] + g2_ref[0] * acc_s[...]
        o_ref[...] = _ln_rows(r) * lng_ref[...] + lnb_ref[...]


def _dense_call(h16, gates, ut3, v16, x2d, mods3, ln_g, ln_b, rows_per_mod, mod_row0, tm, alpha):
    n, d_model = x2d.shape
    tm = _tile(rows_per_mod or n, tm)
    n_blocks, _, te = ut3.shape

    if rows_per_mod is None:
        g2_idx = lambda i, e: (mod_row0 * 6 + 5, 0, 0)
    else:
        g2_idx = lambda i, e: ((mod_row0 + (i * tm) // rows_per_mod) * 6 + 5, 0, 0)
    tok = lambda cols: pl.BlockSpec((tm, cols), lambda i, e: (i, 0))
    const = lambda shape: pl.BlockSpec(shape, lambda i, e: (0,) * len(shape))
    return pl.pallas_call(
        functools.partial(_dense_kernel, alpha=alpha),
        grid=(n // tm, n_blocks),
        in_specs=[
            tok(d_model),
            pl.BlockSpec((tm, te), lambda i, e: (i, e)),
            pl.BlockSpec((1, d_model, te), lambda i, e: (e, 0, 0)),
            pl.BlockSpec((te, d_model), lambda i, e: (e, 0)),
            tok(d_model),
            pl.BlockSpec((1, 1, d_model), g2_idx),
            const((1, d_model)), const((1, d_model)),
        ],
        out_specs=tok(d_model),
        out_shape=jax.ShapeDtypeStruct((n, d_model), F32),
        scratch_shapes=[pltpu.VMEM((tm, d_model), F32)],
        compiler_params=_cparams(("parallel", "arbitrary")),
        name="dense",
    )(h16, gates, ut3, v16, x2d, mods3, ln_g.reshape(1, d_model), ln_b.reshape(1, d_model))


def kernel(x, c, ctx, c_ctx, w_mod, b_mod, w_in, conv_w, conv_b, lb_raw, hg_norm_g, w_pa, w_pb, w_o,
           ln1_g, ln1_b, peer_wq, peer_keys, peer_u, peer_v, ln2_g, ln2_b):
    batch, s_len, d_model = x.shape
    c_len = ctx.shape[1]
    depth = w_mod.shape[0]
    alpha = (2.0 * depth) ** 0.25
    n_lat = batch * s_len
    n_ctx = batch * c_len

    p = jax.nn.softmax(lb_raw.astype(F32), axis=0)
    lower = jnp.cumsum(p, axis=0) - p[:1]

    cond = jnp.concatenate([c, c_ctx[None, :]], axis=0)
    cb = jnp.broadcast_to(cond.T[:, :, None], (d_model, batch + 1, LANES)).reshape(d_model, -1)
    mods = _mod_call(cb, w_mod, b_mod, batch + 1)

    xl = x.reshape(n_lat, d_model)
    xc = ctx.reshape(n_ctx, d_model)
    for l in range(depth):
        last = l == depth - 1
        mods3 = mods[l].reshape(16 * 6, 1, d_model)
        w_in16 = w_in[l].astype(BF16)
        wpa, wpb, wo = w_pa[l].astype(BF16), w_pb[l].astype(BF16), w_o[l].astype(BF16)
        wq = peer_wq[l].astype(BF16)
        keys = peer_keys[l].reshape(PEER_HEADS * 2, N_KEYS, -1).astype(BF16)
        n_exp = peer_u.shape[1]
        ut3 = peer_u[l].astype(BF16).reshape(n_exp // DENSE_TE, DENSE_TE, d_model).transpose(0, 2, 1)
        v16 = peer_v[l].astype(BF16)

        p_lat = _inproj_call(xl, mods3, w_in16, s_len, 0, 1024)
        p_ctx = _inproj_call(xc, mods3, w_in16, None, batch, 1024)
        yb_lat, yb_ctx = _hgrn_call(p_lat, p_ctx, lower[l], hg_norm_g[l], batch, d_model)

        streams = [(xl, p_lat, yb_lat, s_len, 0, GRID_W)]
        if not last:
            streams.append((xc, p_ctx, yb_ctx, None, batch, c_len))
        outs = []
        for xs, ps, ybs, rpm, row0, row_len in streams:
            x1 = _merge_call(ps, ybs, xs, mods3, conv_w[l], conv_b[l], wpa, wpb, wo, ln1_g[l], ln1_b[l],
                             rpm, row0, row_len, 256, alpha)
            h16, ii, jj, gg = _route_call(x1, mods3, wq, keys, rpm, row0, 256)
            gates = _gates_call(ii, jj, gg, 256)
            outs.append(_dense_call(h16, gates, ut3, v16, x1, mods3, ln2_g[l], ln2_b[l],
                                    rpm, row0, 512, alpha))
        xl = outs[0]
        if not last:
            xc = outs[1]
    return xl.reshape(batch, s_len, d_model)
```

```python
import functools
import math

import numpy as np
import jax
import jax.numpy as jnp
from jax import lax
from jax.experimental import pallas as pl
from jax.experimental.pallas import tpu as pltpu

F32 = jnp.float32
BF16 = jnp.bfloat16

LANES = 128
SUBLANES = 8
VMEM_LIMIT = 56 * 1024 * 1024

EPS = 1e-6
F_MIN = 1e-30
GRID_W = 64
HG_DK = 128
HG_DV = 128
CHUNK = 64
LEVELS = (32, 16, 8, 4, 2, 1, 0)
MATMUL_LEVELS = (4, 2)
INTRA_CHUNKS = 4
N_KEYS = 128
TOPK = 16
PEER_HEADS = 8
G_PITCH = 136
DENSE_TE = 1024


def _cparams(sem):
    return pltpu.CompilerParams(dimension_semantics=sem, vmem_limit_bytes=VMEM_LIMIT)


def _tile(n, pref):
    t = min(n, pref)
    while n % t:
        t //= 2
    return t


def _ln_rows(x):
    mu = jnp.mean(x, axis=-1, keepdims=True)
    xc = x - mu
    var = jnp.mean(xc * xc, axis=-1, keepdims=True)
    return xc * lax.rsqrt(var + EPS)


def _sigmoid(x):
    return 1.0 / (1.0 + jnp.exp(-x))


def _silu(x):
    return x * _sigmoid(x)


def _split_bf16(x):
    hi = x.astype(BF16)
    lo = (x - hi.astype(F32)).astype(BF16)
    return hi, lo


def _mod_kernel(cb_ref, w_ref, b_ref, o_ref, s_tab, *, n_rows):
    d_model = w_ref.shape[1]
    tn = w_ref.shape[2]

    @pl.when((pl.program_id(0) == 0) & (pl.program_id(1) == 0))
    def _():
        def fill(g, carry):
            r0 = pl.multiple_of(g * 64, 64)
            s_tab[pl.ds(r0, 64), :] = _silu(cb_ref[pl.ds(r0, 64), :])
            return carry

        lax.fori_loop(0, d_model // 64, fill, 0)

    def body(g, accs):
        r0 = pl.multiple_of(g * SUBLANES, SUBLANES)
        w8 = w_ref[0, pl.ds(r0, SUBLANES), :]
        out = []
        for r, acc in enumerate(accs):
            s8 = s_tab[pl.ds(r0, SUBLANES), r * LANES:(r + 1) * LANES]
            out.append(acc + jnp.concatenate([s8] * (tn // LANES), axis=1) * w8)
        return tuple(out)

    init = tuple(jnp.zeros((SUBLANES, tn), F32) for _ in range(n_rows))
    accs = lax.fori_loop(0, d_model // SUBLANES, body, init, unroll=2)
    bias = b_ref[0]
    o_ref[0] = jnp.zeros((o_ref.shape[1], tn), F32)
    for r, acc in enumerate(accs):
        o_ref[0, r:r + 1, :] = jnp.sum(acc, axis=0, keepdims=True) + bias


def _mod_call(cb, w_mod, b_mod, n_rows):
    depth, d_model, width = w_mod.shape
    tn = 512
    return pl.pallas_call(
        functools.partial(_mod_kernel, n_rows=n_rows),
        grid=(depth, width // tn),
        in_specs=[
            pl.BlockSpec((d_model, n_rows * LANES), lambda l, j: (0, 0), pipeline_mode=pl.Buffered(1)),
            pl.BlockSpec((1, d_model, tn), lambda l, j: (l, 0, j)),
            pl.BlockSpec((1, 1, tn), lambda l, j: (l, 0, j)),
        ],
        out_specs=pl.BlockSpec((1, 16, tn), lambda l, j: (l, 0, j)),
        out_shape=jax.ShapeDtypeStruct((depth, 16, width), F32),
        scratch_shapes=[pltpu.VMEM((d_model, n_rows * LANES), F32)],
        compiler_params=_cparams(("arbitrary", "arbitrary")),
        name="mod",
    )(cb, w_mod, b_mod.reshape(depth, 1, width))


def _inproj_kernel(x_ref, sh_ref, sc_ref, w_ref, o_ref, h_ref):
    @pl.when(pl.program_id(1) == 0)
    def _():
        y = _ln_rows(x_ref[...])
        h_ref[...] = (y * (1.0 + sc_ref[0]) + sh_ref[0]).astype(BF16)

    o_ref[...] = jnp.dot(h_ref[...], w_ref[...], preferred_element_type=F32)


def _inproj_call(x2d, mods3, w_bf16, rows_per_mod, mod_row0, tm):
    n, d_model = x2d.shape
    tm = _tile(rows_per_mod or n, tm)
    width = w_bf16.shape[1]
    tn = _tile(width, 1024)

    def mod_idx(g):
        if rows_per_mod is None:
            return lambda i, j: (mod_row0 * 6 + g, 0, 0)
        return lambda i, j: ((mod_row0 + (i * tm) // rows_per_mod) * 6 + g, 0, 0)

    return pl.pallas_call(
        _inproj_kernel,
        grid=(n // tm, width // tn),
        in_specs=[
            pl.BlockSpec((tm, d_model), lambda i, j: (i, 0)),
            pl.BlockSpec((1, 1, d_model), mod_idx(0)),
            pl.BlockSpec((1, 1, d_model), mod_idx(1)),
            pl.BlockSpec((d_model, tn), lambda i, j: (0, j)),
        ],
        out_specs=pl.BlockSpec((tm, tn), lambda i, j: (i, j)),
        out_shape=jax.ShapeDtypeStruct((n, width), F32),
        scratch_shapes=[pltpu.VMEM((tm, d_model), BF16)],
        compiler_params=_cparams(("parallel", "arbitrary")),
        name="inproj",
    )(x2d, mods3, mods3, w_bf16)


def _decay_matrices(reverse):
    L = CHUNK
    rank = np.arange(L)[::-1] if reverse else np.arange(L)
    rt = rank[:, None]
    rr = rank[None, :]
    blocks = [(rr <= rt)]
    for h in MATMUL_LEVELS:
        upper = (rt & h) != 0
        blocks.append(upper & (rr >= (rt & ~(h - 1))) & (rr <= rt))
        blocks.append((~upper) & (rr > rt) & (rr <= (rt | (h - 1))))
    return np.concatenate(blocks, axis=0).astype(np.float32)


def _half_block_exponents(bcum, h, reverse):
    eq, ek = [], []
    zero = jnp.zeros((h, LANES), F32)
    for b in range(CHUNK // h):
        rows = bcum[b * h:(b + 1) * h]
        first_of_pair = b % 2 == 0
        if not reverse:
            if first_of_pair:
                eq.append(zero)
                ek.append(bcum[(b + 1) * h - 1:(b + 1) * h] - rows)
            else:
                eq.append(rows - bcum[b * h - 1:b * h])
                ek.append(zero)
        else:
            if first_of_pair:
                eq.append(rows - bcum[(b + 1) * h:(b + 1) * h + 1])
                ek.append(zero)
            else:
                eq.append(zero)
                ek.append(bcum[b * h:b * h + 1] - rows)
    return jnp.concatenate(eq, axis=0), jnp.concatenate(ek, axis=0)


def _level_masks(reverse):
    L = CHUNK
    rank = np.arange(L)[::-1] if reverse else np.arange(L)
    rt = rank[:, None]
    rs = rank[None, :]
    out = []
    for h in LEVELS:
        if h == 0:
            out.append(rt == rs)
        else:
            out.append(((rt ^ rs) // h == 1) & (rt > rs))
    return np.stack(out).astype(np.float32)


def _hgrn_intra(chains):
    L = CHUNK
    gs, ks, es, bcums, blasts = [], [], [], [], []
    for qs, z, v16, lb, dmat, masks, reverse in chains:
        sig = _sigmoid(z)
        f = lb + (1.0 - lb) * sig
        g = jnp.log(jnp.maximum(f, F_MIN))
        g_hi, g_lo = _split_bf16(g)
        e2 = jnp.dot(dmat, jnp.concatenate([g_hi, g_lo], axis=1), preferred_element_type=F32)
        e = e2[:, :LANES] + e2[:, LANES:]
        gs.append(g)
        ks.append((1.0 - lb) * (1.0 - sig))
        es.append(e)
        bcums.append(e[0:L])
        blasts.append(e[0:1] if reverse else e[L - 1:L])

    scores = [jnp.zeros((L, L), F32) for _ in chains]
    for i, h in enumerate(LEVELS):
        for c, (qs, z, v16, lb, dmat, masks, reverse) in enumerate(chains):
            k = ks[c]
            if h == 0:
                qt, kt = qs, k
            elif h == 1:
                qt, kt = qs * jnp.exp(gs[c]), k
            elif h in MATMUL_LEVELS:
                m = 1 + 2 * MATMUL_LEVELS.index(h)
                qt = qs * jnp.exp(es[c][m * L:(m + 1) * L])
                kt = k * jnp.exp(es[c][(m + 1) * L:(m + 2) * L])
            else:
                eq, ek = _half_block_exponents(bcums[c], h, reverse)
                qt = qs * jnp.exp(eq)
                kt = k * jnp.exp(ek)
            s_h = lax.dot_general(qt.astype(BF16), kt.astype(BF16), (((1,), (1,)), ((), ())),
                                  preferred_element_type=F32)
            scores[c] = scores[c] + masks[i] * s_h

    outs = []
    for c, (qs, z, v16, lb, dmat, masks, reverse) in enumerate(chains):
        o = jnp.dot(scores[c].astype(BF16), v16, preferred_element_type=F32)
        qhat = (qs * jnp.exp(bcums[c])).astype(BF16)
        kvt = lax.dot_general(v16, (ks[c] * jnp.exp(blasts[c] - bcums[c])).astype(BF16),
                              (((0,), (0,)), ((), ())), preferred_element_type=F32)
        outs.append((o, qhat, kvt, blasts[c]))
    return outs


def _hgrn_kernel(ql, zfl, zbl, vl, ogl, qc, zfc, zbc, vc, ogc, lb_ref, ng_ref, dm_ref, mk_ref,
                 yl_ref, yc_ref, ol_s, oc_s, qh_s, kv_s, bl_s, st_s):
    lbf = lb_ref[0:1, :]
    lbb = lb_ref[1:2, :]

    def run(q_ref, zf_ref, zb_ref, v_ref, o_s, states):
        n = q_ref.shape[0] // CHUNK
        per_it = math.gcd(n, INTRA_CHUNKS)

        def intra(it, carry):
            chains = []
            for u in range(per_it):
                c = it * per_it + u
                r = pl.multiple_of(c * CHUNK, CHUNK)
                qs = _silu(q_ref[pl.ds(r, CHUNK), :]) * (HG_DK ** -0.5)
                v16 = v_ref[pl.ds(r, CHUNK), :].astype(BF16)
                chains.append((qs, zf_ref[pl.ds(r, CHUNK), :], v16, lbf, dm_ref[0], mk_ref[0], False))
                chains.append((qs, zb_ref[pl.ds(r, CHUNK), :], v16, lbb, dm_ref[1], mk_ref[1], True))
            res = _hgrn_intra(chains)
            for u in range(per_it):
                c = it * per_it + u
                r = pl.multiple_of(c * CHUNK, CHUNK)
                (of, qf, kvf, blf), (ob, qb, kvb, blb) = res[2 * u], res[2 * u + 1]
                o_s[pl.ds(r, CHUNK), :] = of + ob
                qh_s[pl.ds(r, CHUNK), 0:HG_DK] = qf
                qh_s[pl.ds(r, CHUNK), HG_DK:2 * HG_DK] = qb
                kv_s[0, c] = kvf
                kv_s[1, c] = kvb
                bl_s[0, c] = jnp.broadcast_to(blf, (SUBLANES, HG_DK))
                bl_s[1, c] = jnp.broadcast_to(blb, (SUBLANES, HG_DK))
            return carry

        lax.fori_loop(0, n // per_it, intra, 0)

        def recur(c, carry):
            sf, sb = carry
            cb = n - 1 - c
            st_s[c, :, 0:HG_DK] = sf.astype(BF16)
            st_s[cb, :, HG_DK:2 * HG_DK] = sb.astype(BF16)
            sf = sf * jnp.exp(bl_s[0, c][0:1, :]) + kv_s[0, c]
            sb = sb * jnp.exp(bl_s[1, cb][0:1, :]) + kv_s[1, cb]
            return sf, sb

        states = lax.fori_loop(0, n, recur, states)

        def inter(c, carry):
            r = pl.multiple_of(c * CHUNK, CHUNK)
            o_s[pl.ds(r, CHUNK), :] += lax.dot_general(
                qh_s[pl.ds(r, CHUNK), :], st_s[c], (((1,), (1,)), ((), ())), preferred_element_type=F32)
            return carry

        lax.fori_loop(0, n, inter, 0, unroll=4)
        return states

    zero = jnp.zeros((HG_DV, HG_DK), F32)
    states = run(qc, zfc, zbc, vc, oc_s, (zero, zero))
    run(ql, zfl, zbl, vl, ol_s, states)

    def readout(o_s, og_ref, y_ref):
        o = o_s[...]
        o = o * lax.rsqrt(jnp.mean(o * o, axis=-1, keepdims=True) + EPS) * ng_ref[...]
        y_ref[...] = (o * _silu(og_ref[...])).astype(y_ref.dtype)

    readout(ol_s, ogl, yl_ref)
    readout(oc_s, ogc, yc_ref)


def _hgrn_call(p_lat, p_ctx, lb, norm_g, batch, d_model):
    n_lat = p_lat.shape[0]
    n_ctx = p_ctx.shape[0]
    s_len = n_lat // batch
    c_len = n_ctx // batch
    d_half = d_model // 2
    heads = d_half // HG_DK
    cb = d_half // LANES
    q0, zf0, zb0, v0, og0 = 3 * cb, 4 * cb, 5 * cb, 6 * cb, 7 * cb

    def spec(rows, c0):
        return pl.BlockSpec((rows, LANES), lambda b, h, c0=c0: (b, c0 + h))

    dmat = jnp.asarray(np.stack([_decay_matrices(False), _decay_matrices(True)]), BF16)
    masks = jnp.asarray(np.stack([_level_masks(False), _level_masks(True)]), F32)
    full = lambda shape: pl.BlockSpec(shape, lambda b, h: (0,) * len(shape))

    return pl.pallas_call(
        _hgrn_kernel,
        grid=(batch, heads),
        in_specs=[spec(s_len, c0) for c0 in (q0, zf0, zb0, v0, og0)]
        + [spec(c_len, c0) for c0 in (q0, zf0, zb0, v0, og0)]
        + [pl.BlockSpec((2, LANES), lambda b, h: (0, h)),
           full((1, HG_DV)), full(dmat.shape), full(masks.shape)],
        out_specs=[pl.BlockSpec((s_len, LANES), lambda b, h: (b, h)),
                   pl.BlockSpec((c_len, LANES), lambda b, h: (b, h))],
        out_shape=[jax.ShapeDtypeStruct((n_lat, d_half), BF16),
                   jax.ShapeDtypeStruct((n_ctx, d_half), BF16)],
        scratch_shapes=[pltpu.VMEM((s_len, HG_DV), F32), pltpu.VMEM((c_len, HG_DV), F32),
                        pltpu.VMEM((s_len, 2 * HG_DK), BF16),
                        pltpu.VMEM((2, s_len // CHUNK, HG_DV, HG_DK), F32),
                        pltpu.VMEM((2, s_len // CHUNK, SUBLANES, HG_DK), F32),
                        pltpu.VMEM((s_len // CHUNK, HG_DV, 2 * HG_DK), BF16)],
        compiler_params=_cparams(("parallel", "parallel")),
        name="hgrn",
    )(*([p_lat] * 5), *([p_ctx] * 5), lb, norm_g.reshape(1, HG_DV), dmat, masks)


def _merge_kernel(cb_ref, cc_ref, cv_ref, ga_ref, gb_ref, yb_ref, x_ref, g1_ref, cw_ref, cbias_ref,
                  wpa_ref, wpb_ref, wo_ref, lng_ref, lnb_ref, o_ref, *, row_len, alpha):
    tm = x_ref.shape[0]
    u = cc_ref[...] * cv_ref[...]
    pos = lax.broadcasted_iota(jnp.int32, (tm, 1), 0) % row_len
    prev = jnp.where(pos == 0, 0.0, pltpu.roll(u, 1, axis=0))
    nxt = jnp.where(pos == row_len - 1, 0.0, pltpu.roll(u, tm - 1, axis=0))
    conv = prev * cw_ref[0:1, :] + u * cw_ref[1:2, :] + nxt * cw_ref[2:3, :] + cbias_ref[...]
    ya = (cb_ref[...] * conv).astype(BF16)
    m = (_sigmoid(ga_ref[...]) * jnp.dot(ya, wpa_ref[...], preferred_element_type=F32)
         + _sigmoid(gb_ref[...]) * jnp.dot(yb_ref[...], wpb_ref[...], preferred_element_type=F32))
    y = jnp.dot(m.astype(BF16), wo_ref[...], preferred_element_type=F32)
    r = alpha * x_ref[...] + g1_ref[0] * y
    o_ref[...] = _ln_rows(r) * lng_ref[...] + lnb_ref[...]


def _merge_call(p, yb, x2d, mods3, conv_w, conv_b, wpa, wpb, wo, ln_g, ln_b, rows_per_mod, mod_row0,
                row_len, tm, alpha):
    n, d_model = x2d.shape
    tm = _tile(rows_per_mod or n, tm)
    d_half = d_model // 2

    if rows_per_mod is None:
        g1_idx = lambda i: (mod_row0 * 6 + 2, 0, 0)
    else:
        g1_idx = lambda i: ((mod_row0 + (i * tm) // rows_per_mod) * 6 + 2, 0, 0)
    const = lambda shape: pl.BlockSpec(shape, lambda i: (0,) * len(shape), pipeline_mode=pl.Buffered(1))
    return pl.pallas_call(
        functools.partial(_merge_kernel, row_len=row_len, alpha=alpha),
        grid=(n // tm,),
        in_specs=[
            pl.BlockSpec((tm, d_half), lambda i: (i, 0)),
            pl.BlockSpec((tm, d_half), lambda i: (i, 1)),
            pl.BlockSpec((tm, d_half), lambda i: (i, 2)),
            pl.BlockSpec((tm, d_model), lambda i: (i, 4)),
            pl.BlockSpec((tm, d_model), lambda i: (i, 5)),
            pl.BlockSpec((tm, d_half), lambda i: (i, 0)),
            pl.BlockSpec((tm, d_model), lambda i: (i, 0)),
            pl.BlockSpec((1, 1, d_model), g1_idx),
            const((3, d_half)), const((1, d_half)),
            const((d_half, d_model)), const((d_half, d_model)), const((d_model, d_model)),
            const((1, d_model)), const((1, d_model)),
        ],
        out_specs=pl.BlockSpec((tm, d_model), lambda i: (i, 0)),
        out_shape=jax.ShapeDtypeStruct((n, d_model), F32),
        compiler_params=_cparams(("parallel",)),
        name="merge",
    )(p, p, p, p, p, yb, x2d, mods3, conv_w, conv_b.reshape(1, d_half), wpa, wpb, wo,
      ln_g.reshape(1, d_model), ln_b.reshape(1, d_model))


def _topk_rows(s, k, payload=None):
    rows = s.shape[0]
    riota = lax.broadcasted_iota(jnp.int32, s.shape, 0).astype(F32)
    vals, picks = [], []
    for _ in range(k):
        m = jnp.max(s, axis=0, keepdims=True)
        am = jnp.min(jnp.where(s == m, riota, float(rows)), axis=0, keepdims=True)
        sel = riota == am
        vals.append(m)
        if payload is None:
            picks.append(am)
        else:
            picks.append(jnp.max(jnp.where(sel, payload, -1.0), axis=0, keepdims=True))
        s = jnp.where(sel, -jnp.inf, s)
    return jnp.concatenate(vals, axis=0), jnp.concatenate(picks, axis=0)


def _pair_candidates():
    return [(a, min(TOPK, TOPK // (a + 1))) for a in range(TOPK)]


def _route_kernel(x_ref, sh_ref, sc_ref, wq_ref, keys_ref, h_ref, ii_ref, jj_ref, gg_ref):
    hmod = _ln_rows(x_ref[...]) * (1.0 + sc_ref[0]) + sh_ref[0]
    h16 = hmod.astype(BF16)
    h_ref[...] = h16
    qp = jnp.dot(h16, wq_ref[...], preferred_element_type=F32).astype(BF16)
    i_rows, j_rows, g_rows = [], [], []
    for head in range(PEER_HEADS):
        sv, si = [], []
        for half in range(2):
            c = (head * 2 + half) * LANES
            s = lax.dot_general(keys_ref[head * 2 + half], qp[:, c:c + LANES], (((1,), (1,)), ((), ())),
                                preferred_element_type=F32)
            v_, i_ = _topk_rows(s, TOPK)
            sv.append(v_)
            si.append(i_)
        pairs = _pair_candidates()
        n_pad = -sum(nb for _, nb in pairs) % SUBLANES
        tm = s.shape[1]
        cand = jnp.concatenate([sv[0][a:a + 1, :] + sv[1][0:nb, :] for a, nb in pairs]
                               + [jnp.full((n_pad, tm), -jnp.inf, F32)], axis=0)
        cidx = jnp.concatenate([si[0][a:a + 1, :] * float(N_KEYS) + si[1][0:nb, :] for a, nb in pairs]
                               + [jnp.zeros((n_pad, tm), F32)], axis=0)
        tv, te = _topk_rows(cand, TOPK, payload=cidx)
        te = te.astype(jnp.int32)
        ex = jnp.exp(tv - tv[0:1, :])
        g_rows.append(ex / jnp.sum(ex, axis=0, keepdims=True))
        i_rows.append(lax.shift_right_logical(te, 7))
        j_rows.append(te & (N_KEYS - 1))
    ii_ref[...] = jnp.concatenate(i_rows, axis=0).T
    jj_ref[...] = jnp.concatenate(j_rows, axis=0).T
    gg_ref[...] = jnp.concatenate(g_rows, axis=0).T


def _route_call(x2d, mods3, wq, keys, rows_per_mod, mod_row0, tm):
    n, d_model = x2d.shape
    tm = _tile(rows_per_mod or n, tm)
    slots = PEER_HEADS * TOPK

    def mod_idx(g):
        if rows_per_mod is None:
            return lambda i: (mod_row0 * 6 + g, 0, 0)
        return lambda i: ((mod_row0 + (i * tm) // rows_per_mod) * 6 + g, 0, 0)

    const = lambda shape: pl.BlockSpec(shape, lambda i: (0,) * len(shape), pipeline_mode=pl.Buffered(1))
    return pl.pallas_call(
        _route_kernel,
        grid=(n // tm,),
        in_specs=[
            pl.BlockSpec((tm, d_model), lambda i: (i, 0)),
            pl.BlockSpec((1, 1, d_model), mod_idx(3)),
            pl.BlockSpec((1, 1, d_model), mod_idx(4)),
            const(wq.shape), const(keys.shape),
        ],
        out_specs=[pl.BlockSpec((tm, d_model), lambda i: (i, 0))]
        + [pl.BlockSpec((tm, slots), lambda i: (i, 0))] * 3,
        out_shape=[jax.ShapeDtypeStruct((n, d_model), BF16),
                   jax.ShapeDtypeStruct((n, slots), jnp.int32),
                   jax.ShapeDtypeStruct((n, slots), jnp.int32),
                   jax.ShapeDtypeStruct((n, slots), F32)],
        compiler_params=_cparams(("parallel",)),
        name="route",
    )(x2d, mods3, mods3, wq, keys)


def _gelu_exact(x):
    return 0.5 * x * (1.0 + lax.erf(x * (2.0 ** -0.5)))


def _gates_kernel(ii_ref, jj_ref, gg_ref, o_ref, gate_s):
    tm = ii_ref.shape[0]
    sub = lax.broadcasted_iota(jnp.int32, (N_KEYS, N_KEYS), 0)

    def token(n, carry):
        ii = ii_ref[pl.ds(n, 1), :]
        jj = jj_ref[pl.ds(n, 1), :]
        gg = gg_ref[pl.ds(n, 1), :]
        a_t = jnp.where(ii == sub, gg, 0.0).astype(BF16)
        b_t = jnp.where(jj == sub, 1.0, 0.0).astype(BF16)
        gate = lax.dot_general(a_t, b_t, (((1,), (1,)), ((), ())), preferred_element_type=F32)
        gate_s[pl.ds(pl.multiple_of(n * G_PITCH, SUBLANES), N_KEYS), :] = gate
        return carry

    lax.fori_loop(0, tm, token, 0, unroll=16)
    for i in range(N_KEYS):
        o_ref[:, i * N_KEYS:(i + 1) * N_KEYS] = gate_s[pl.ds(i, tm, stride=G_PITCH), :].astype(o_ref.dtype)


def _gates_call(ii, jj, gg, tm):
    n, slots = ii.shape
    tm = _tile(n, tm)
    tok = pl.BlockSpec((tm, slots), lambda i: (i, 0))
    return pl.pallas_call(
        _gates_kernel,
        grid=(n // tm,),
        in_specs=[tok, tok, tok],
        out_specs=pl.BlockSpec((tm, N_KEYS * N_KEYS), lambda i: (i, 0)),
        out_shape=jax.ShapeDtypeStruct((n, N_KEYS * N_KEYS), BF16),
        scratch_shapes=[pltpu.VMEM((tm * G_PITCH, LANES), F32)],
        compiler_params=_cparams(("parallel",)),
        name="gates",
    )(ii, jj, gg)


def _dense_kernel(h_ref, gate_ref, ut_ref, v_ref, x_ref, g2_ref, lng_ref, lnb_ref, o_ref, acc_s, *, alpha):
    eb = pl.program_id(1)

    @pl.when(eb == 0)
    def _():
        acc_s[...] = jnp.zeros(acc_s.shape, F32)

    act = _gelu_exact(jnp.dot(h_ref[...], ut_ref[0], preferred_element_type=F32))
    hg = (gate_ref[...].astype(F32) * act).astype(BF16)
    acc_s[...] += jnp.dot(hg, v_ref[...], preferred_element_type=F32)

    @pl.when(eb == pl.num_programs(1) - 1)
    def _():
        r = alpha * x_ref[...] + g2_ref[0] * acc_s[...]
        o_ref[...] = _ln_rows(r) * lng_ref[...] + lnb_ref[...]


def _dense_call(h16, gates, ut3, v16, x2d, mods3, ln_g, ln_b, rows_per_mod, mod_row0, tm, alpha):
    n, d_model = x2d.shape
    tm = _tile(rows_per_mod or n, tm)
    n_blocks, _, te = ut3.shape

    if rows_per_mod is None:
        g2_idx = lambda i, e: (mod_row0 * 6 + 5, 0, 0)
    else:
        g2_idx = lambda i, e: ((mod_row0 + (i * tm) // rows_per_mod) * 6 + 5, 0, 0)
    tok = lambda cols: pl.BlockSpec((tm, cols), lambda i, e: (i, 0))
    const = lambda shape: pl.BlockSpec(shape, lambda i, e: (0,) * len(shape))
    return pl.pallas_call(
        functools.partial(_dense_kernel, alpha=alpha),
        grid=(n // tm, n_blocks),
        in_specs=[
            tok(d_model),
            pl.BlockSpec((tm, te), lambda i, e: (i, e)),
            pl.BlockSpec((1, d_model, te), lambda i, e: (e, 0, 0)),
            pl.BlockSpec((te, d_model), lambda i, e: (e, 0)),
            tok(d_model),
            pl.BlockSpec((1, 1, d_model), g2_idx),
            const((1, d_model)), const((1, d_model)),
        ],
        out_specs=tok(d_model),
        out_shape=jax.ShapeDtypeStruct((n, d_model), F32),
        scratch_shapes=[pltpu.VMEM((tm, d_model), F32)],
        compiler_params=_cparams(("parallel", "arbitrary")),
        name="dense",
    )(h16, gates, ut3, v16, x2d, mods3, ln_g.reshape(1, d_model), ln_b.reshape(1, d_model))


def kernel(x, c, ctx, c_ctx, w_mod, b_mod, w_in, conv_w, conv_b, lb_raw, hg_norm_g, w_pa, w_pb, w_o,
           ln1_g, ln1_b, peer_wq, peer_keys, peer_u, peer_v, ln2_g, ln2_b):
    batch, s_len, d_model = x.shape
    c_len = ctx.shape[1]
    depth = w_mod.shape[0]
    alpha = (2.0 * depth) ** 0.25
    n_lat = batch * s_len
    n_ctx = batch * c_len

    p = jax.nn.softmax(lb_raw.astype(F32), axis=0)
    lower = jnp.cumsum(p, axis=0) - p[:1]

    cond = jnp.concatenate([c, c_ctx[None, :]], axis=0)
    cb = jnp.broadcast_to(cond.T[:, :, None], (d_model, batch + 1, LANES)).reshape(d_model, -1)
    mods = _mod_call(cb, w_mod, b_mod, batch + 1)

    xl = x.reshape(n_lat, d_model)
    xc = ctx.reshape(n_ctx, d_model)
    for l in range(depth):
        last = l == depth - 1
        mods3 = mods[l].reshape(16 * 6, 1, d_model)
        w_in16 = w_in[l].astype(BF16)
        wpa, wpb, wo = w_pa[l].astype(BF16), w_pb[l].astype(BF16), w_o[l].astype(BF16)
        wq = peer_wq[l].astype(BF16)
        keys = peer_keys[l].reshape(PEER_HEADS * 2, N_KEYS, -1).astype(BF16)
        n_exp = peer_u.shape[1]
        ut3 = peer_u[l].astype(BF16).reshape(n_exp // DENSE_TE, DENSE_TE, d_model).transpose(0, 2, 1)
        v16 = peer_v[l].astype(BF16)

        p_lat = _inproj_call(xl, mods3, w_in16, s_len, 0, 1024)
        p_ctx = _inproj_call(xc, mods3, w_in16, None, batch, 1024)
        yb_lat, yb_ctx = _hgrn_call(p_lat, p_ctx, lower[l], hg_norm_g[l], batch, d_model)

        streams = [(xl, p_lat, yb_lat, s_len, 0, GRID_W)]
        if not last:
            streams.append((xc, p_ctx, yb_ctx, None, batch, c_len))
        outs = []
        for xs, ps, ybs, rpm, row0, row_len in streams:
            x1 = _merge_call(ps, ybs, xs, mods3, conv_w[l], conv_b[l], wpa, wpb, wo, ln1_g[l], ln1_b[l],
                             rpm, row0, row_len, 256, alpha)
            h16, ii, jj, gg = _route_call(x1, mods3, wq, keys, rpm, row0, 256)
            gates = _gates_call(ii, jj, gg, 256)
            outs.append(_dense_call(h16, gates, ut3, v16, x1, mods3, ln2_g[l], ln2_b[l],
                                    rpm, row0, 512, alpha))
        xl = outs[0]
        if not last:
            xc = outs[1]
    return xl.reshape(batch, s_len, d_model)
```

```python
import functools
import math

import numpy as np
import jax
import jax.numpy as jnp
from jax import lax
from jax.experimental import pallas as pl
from jax.experimental.pallas import tpu as pltpu

F32 = jnp.float32
BF16 = jnp.bfloat16

LANES = 128
SUBLANES = 8
VMEM_LIMIT = 56 * 1024 * 1024

EPS = 1e-6
F_MIN = 1e-30
GRID_W = 64
HG_DK = 128
HG_DV = 128
CHUNK = 64
LEVELS = (32, 16, 8, 4, 2, 1, 0)
MATMUL_LEVELS = (4, 2)
INTRA_CHUNKS = 4
N_KEYS = 128
TOPK = 16
PEER_HEADS = 8
G_PITCH = 136
DENSE_TE = 1024
DENSE_TM = 512
CAST_BLOCK_BYTES = 4 * 1024 * 1024


def _cparams(sem):
    return pltpu.CompilerParams(dimension_semantics=sem, vmem_limit_bytes=VMEM_LIMIT)


def _tile(n, pref):
    t = min(n, pref)
    while n % t:
        t //= 2
    return t


def _ln_rows(x):
    mu = jnp.mean(x, axis=-1, keepdims=True)
    xc = x - mu
    var = jnp.mean(xc * xc, axis=-1, keepdims=True)
    return xc * lax.rsqrt(var + EPS)


def _sigmoid(x):
    return 1.0 / (1.0 + jnp.exp(-x))


def _silu(x):
    return x * _sigmoid(x)


def _split_bf16(x):
    hi = x.astype(BF16)
    lo = (x - hi.astype(F32)).astype(BF16)
    return hi, lo


def _cast_kernel(x_ref, o_ref):
    o_ref[...] = x_ref[...].astype(o_ref.dtype)


def _cast_call(w):
    depth, r, c = w.shape
    rows = _tile(r, 1 << max(4, (CAST_BLOCK_BYTES // (4 * c)).bit_length() - 1))
    spec = pl.BlockSpec((1, rows, c), lambda l, i: (l, i, 0))
    return pl.pallas_call(
        _cast_kernel, grid=(depth, r // rows), in_specs=[spec], out_specs=spec,
        out_shape=jax.ShapeDtypeStruct(w.shape, BF16),
        compiler_params=_cparams(("parallel", "parallel")), name="cast",
    )(w)


def _cast_t_kernel(x_ref, o_ref):
    d = x_ref.shape[2]
    step = 256 if d % 256 == 0 else d
    for c in range(0, d, step):
        o_ref[0, 0, c:c + step, :] = x_ref[0, :, c:c + step].T.astype(o_ref.dtype)


def _cast_t_call(w, te):
    depth, e, d = w.shape
    return pl.pallas_call(
        _cast_t_kernel, grid=(depth, e // te),
        in_specs=[pl.BlockSpec((1, te, d), lambda l, i: (l, i, 0))],
        out_specs=pl.BlockSpec((1, 1, d, te), lambda l, i: (l, i, 0, 0)),
        out_shape=jax.ShapeDtypeStruct((depth, e // te, d, te), BF16),
        compiler_params=_cparams(("parallel", "parallel")), name="cast_t",
    )(w)


def _mod_kernel(cb_ref, w_ref, b_ref, o_ref, s_tab, *, n_rows):
    d_model = w_ref.shape[1]
    tn = w_ref.shape[2]

    @pl.when((pl.program_id(0) == 0) & (pl.program_id(1) == 0))
    def _():
        def fill(g, carry):
            r0 = pl.multiple_of(g * 64, 64)
            s_tab[pl.ds(r0, 64), :] = _silu(cb_ref[pl.ds(r0, 64), :])
            return carry

        lax.fori_loop(0, d_model // 64, fill, 0)

    def body(g, accs):
        r0 = pl.multiple_of(g * SUBLANES, SUBLANES)
        w8 = w_ref[0, pl.ds(r0, SUBLANES), :]
        out = []
        for r, acc in enumerate(accs):
            s8 = s_tab[pl.ds(r0, SUBLANES), r * LANES:(r + 1) * LANES]
            out.append(acc + jnp.concatenate([s8] * (tn // LANES), axis=1) * w8)
        return tuple(out)

    init = tuple(jnp.zeros((SUBLANES, tn), F32) for _ in range(n_rows))
    accs = lax.fori_loop(0, d_model // SUBLANES, body, init, unroll=2)
    bias = b_ref[0]
    o_ref[0] = jnp.zeros((o_ref.shape[1], tn), F32)
    for r, acc in enumerate(accs):
        o_ref[0, r:r + 1, :] = jnp.sum(acc, axis=0, keepdims=True) + bias


def _mod_call(cb, w_mod, b_mod, n_rows):
    depth, d_model, width = w_mod.shape
    tn = 512
    return pl.pallas_call(
        functools.partial(_mod_kernel, n_rows=n_rows),
        grid=(depth, width // tn),
        in_specs=[
            pl.BlockSpec((d_model, n_rows * LANES), lambda l, j: (0, 0), pipeline_mode=pl.Buffered(1)),
            pl.BlockSpec((1, d_model, tn), lambda l, j: (l, 0, j)),
            pl.BlockSpec((1, 1, tn), lambda l, j: (l, 0, j)),
        ],
        out_specs=pl.BlockSpec((1, 16, tn), lambda l, j: (l, 0, j)),
        out_shape=jax.ShapeDtypeStruct((depth, 16, width), F32),
        scratch_shapes=[pltpu.VMEM((d_model, n_rows * LANES), F32)],
        compiler_params=_cparams(("arbitrary", "arbitrary")),
        name="mod",
    )(cb, w_mod, b_mod.reshape(depth, 1, width))


def _inproj_kernel(x_ref, sh_ref, sc_ref, w_ref, o_ref, h_ref):
    @pl.when(pl.program_id(1) == 0)
    def _():
        y = _ln_rows(x_ref[...])
        h_ref[...] = (y * (1.0 + sc_ref[0]) + sh_ref[0]).astype(BF16)

    o_ref[...] = jnp.dot(h_ref[...], w_ref[0], preferred_element_type=F32)


def _inproj_call(x2d, mods3, w_bf16, layer, rows_per_mod, mod_row0, tm):
    n, d_model = x2d.shape
    tm = _tile(rows_per_mod or n, tm)
    width = w_bf16.shape[2]
    tn = _tile(width, 1024)

    def mod_idx(g):
        if rows_per_mod is None:
            return lambda i, j: (mod_row0 * 6 + g, 0, 0)
        return lambda i, j: ((mod_row0 + (i * tm) // rows_per_mod) * 6 + g, 0, 0)

    return pl.pallas_call(
        _inproj_kernel,
        grid=(n // tm, width // tn),
        in_specs=[
            pl.BlockSpec((tm, d_model), lambda i, j: (i, 0)),
            pl.BlockSpec((1, 1, d_model), mod_idx(0)),
            pl.BlockSpec((1, 1, d_model), mod_idx(1)),
            pl.BlockSpec((1, d_model, tn), lambda i, j: (layer, 0, j)),
        ],
        out_specs=pl.BlockSpec((tm, tn), lambda i, j: (i, j)),
        out_shape=jax.ShapeDtypeStruct((n, width), F32),
        scratch_shapes=[pltpu.VMEM((tm, d_model), BF16)],
        compiler_params=_cparams(("parallel", "arbitrary")),
        name="inproj",
    )(x2d, mods3, mods3, w_bf16)


def _decay_matrices(reverse):
    L = CHUNK
    rank = np.arange(L)[::-1] if reverse else np.arange(L)
    rt = rank[:, None]
    rr = rank[None, :]
    blocks = [(rr <= rt)]
    for h in MATMUL_LEVELS:
        upper = (rt & h) != 0
        blocks.append(upper & (rr >= (rt & ~(h - 1))) & (rr <= rt))
        blocks.append((~upper) & (rr > rt) & (rr <= (rt | (h - 1))))
    return np.concatenate(blocks, axis=0).astype(np.float32)


def _half_block_exponents(bcum, h, reverse):
    eq, ek = [], []
    zero = jnp.zeros((h, LANES), F32)
    for b in range(CHUNK // h):
        rows = bcum[b * h:(b + 1) * h]
        first_of_pair = b % 2 == 0
        if not reverse:
            if first_of_pair:
                eq.append(zero)
                ek.append(bcum[(b + 1) * h - 1:(b + 1) * h] - rows)
            else:
                eq.append(rows - bcum[b * h - 1:b * h])
                ek.append(zero)
        else:
            if first_of_pair:
                eq.append(rows - bcum[(b + 1) * h:(b + 1) * h + 1])
                ek.append(zero)
            else:
                eq.append(zero)
                ek.append(bcum[b * h:b * h + 1] - rows)
    return jnp.concatenate(eq, axis=0), jnp.concatenate(ek, axis=0)


def _level_masks(reverse):
    L = CHUNK
    rank = np.arange(L)[::-1] if reverse else np.arange(L)
    rt = rank[:, None]
    rs = rank[None, :]
    out = []
    for h in LEVELS:
        if h == 0:
            out.append(rt == rs)
        else:
            out.append(((rt ^ rs) // h == 1) & (rt > rs))
    return np.stack(out).astype(np.float32)


def _hgrn_intra(chains):
    L = CHUNK
    gs, ks, es, bcums, blasts = [], [], [], [], []
    for qs, z, v16, lb, dmat, masks, reverse in chains:
        sig = _sigmoid(z)
        f = lb + (1.0 - lb) * sig
        g = jnp.log(jnp.maximum(f, F_MIN))
        g_hi, g_lo = _split_bf16(g)
        e2 = jnp.dot(dmat, jnp.concatenate([g_hi, g_lo], axis=1), preferred_element_type=F32)
        e = e2[:, :LANES] + e2[:, LANES:]
        gs.append(g)
        ks.append((1.0 - lb) * (1.0 - sig))
        es.append(e)
        bcums.append(e[0:L])
        blasts.append(e[0:1] if reverse else e[L - 1:L])

    scores = [jnp.zeros((L, L), F32) for _ in chains]
    for i, h in enumerate(LEVELS):
        for c, (qs, z, v16, lb, dmat, masks, reverse) in enumerate(chains):
            k = ks[c]
            if h == 0:
                qt, kt = qs, k
            elif h == 1:
                qt, kt = qs * jnp.exp(gs[c]), k
            elif h in MATMUL_LEVELS:
                m = 1 + 2 * MATMUL_LEVELS.index(h)
                qt = qs * jnp.exp(es[c][m * L:(m + 1) * L])
                kt = k * jnp.exp(es[c][(m + 1) * L:(m + 2) * L])
            else:
                eq, ek = _half_block_exponents(bcums[c], h, reverse)
                qt = qs * jnp.exp(eq)
                kt = k * jnp.exp(ek)
            s_h = lax.dot_general(qt.astype(BF16), kt.astype(BF16), (((1,), (1,)), ((), ())),
                                  preferred_element_type=F32)
            scores[c] = scores[c] + masks[i] * s_h

    outs = []
    for c, (qs, z, v16, lb, dmat, masks, reverse) in enumerate(chains):
        o = jnp.dot(scores[c].astype(BF16), v16, preferred_element_type=F32)
        qhat = (qs * jnp.exp(bcums[c])).astype(BF16)
        kvt = lax.dot_general(v16, (ks[c] * jnp.exp(blasts[c] - bcums[c])).astype(BF16),
                              (((0,), (0,)), ((), ())), preferred_element_type=F32)
        outs.append((o, qhat, kvt, blasts[c]))
    return outs


def _hgrn_kernel(ql, zfl, zbl, vl, ogl, qc, zfc, zbc, vc, ogc, lb_ref, ng_ref, dm_ref, mk_ref,
                 yl_ref, yc_ref, ol_s, oc_s, qh_s, kv_s, bl_s, st_s):
    lbf = lb_ref[0:1, :]
    lbb = lb_ref[1:2, :]

    def run(q_ref, zf_ref, zb_ref, v_ref, o_s, states):
        n = q_ref.shape[0] // CHUNK
        per_it = math.gcd(n, INTRA_CHUNKS)

        def intra(it, carry):
            chains = []
            for u in range(per_it):
                c = it * per_it + u
                r = pl.multiple_of(c * CHUNK, CHUNK)
                qs = _silu(q_ref[pl.ds(r, CHUNK), :]) * (HG_DK ** -0.5)
                v16 = v_ref[pl.ds(r, CHUNK), :].astype(BF16)
                chains.append((qs, zf_ref[pl.ds(r, CHUNK), :], v16, lbf, dm_ref[0], mk_ref[0], False))
                chains.append((qs, zb_ref[pl.ds(r, CHUNK), :], v16, lbb, dm_ref[1], mk_ref[1], True))
            res = _hgrn_intra(chains)
            for u in range(per_it):
                c = it * per_it + u
                r = pl.multiple_of(c * CHUNK, CHUNK)
                (of, qf, kvf, blf), (ob, qb, kvb, blb) = res[2 * u], res[2 * u + 1]
                o_s[pl.ds(r, CHUNK), :] = of + ob
                qh_s[pl.ds(r, CHUNK), 0:HG_DK] = qf
                qh_s[pl.ds(r, CHUNK), HG_DK:2 * HG_DK] = qb
                kv_s[0, c] = kvf
                kv_s[1, c] = kvb
                bl_s[0, c] = jnp.broadcast_to(blf, (SUBLANES, HG_DK))
                bl_s[1, c] = jnp.broadcast_to(blb, (SUBLANES, HG_DK))
            return carry

        lax.fori_loop(0, n // per_it, intra, 0)

        def recur(c, carry):
            sf, sb = carry
            cb = n - 1 - c
            st_s[c, :, 0:HG_DK] = sf.astype(BF16)
            st_s[cb, :, HG_DK:2 * HG_DK] = sb.astype(BF16)
            sf = sf * jnp.exp(bl_s[0, c][0:1, :]) + kv_s[0, c]
            sb = sb * jnp.exp(bl_s[1, cb][0:1, :]) + kv_s[1, cb]
            return sf, sb

        states = lax.fori_loop(0, n, recur, states)

        def inter(c, carry):
            r = pl.multiple_of(c * CHUNK, CHUNK)
            o_s[pl.ds(r, CHUNK), :] += lax.dot_general(
                qh_s[pl.ds(r, CHUNK), :], st_s[c], (((1,), (1,)), ((), ())), preferred_element_type=F32)
            return carry

        lax.fori_loop(0, n, inter, 0, unroll=4)
        return states

    zero = jnp.zeros((HG_DV, HG_DK), F32)
    states = run(qc, zfc, zbc, vc, oc_s, (zero, zero))
    run(ql, zfl, zbl, vl, ol_s, states)

    def readout(o_s, og_ref, y_ref):
        o = o_s[...]
        o = o * lax.rsqrt(jnp.mean(o * o, axis=-1, keepdims=True) + EPS) * ng_ref[...]
        y_ref[...] = (o * _silu(og_ref[...])).astype(y_ref.dtype)

    readout(ol_s, ogl, yl_ref)
    readout(oc_s, ogc, yc_ref)


def _hgrn_call(p_lat, p_ctx, lb, norm_g, batch, d_model):
    n_lat = p_lat.shape[0]
    n_ctx = p_ctx.shape[0]
    s_len = n_lat // batch
    c_len = n_ctx // batch
    d_half = d_model // 2
    heads = d_half // HG_DK
    cb = d_half // LANES
    q0, zf0, zb0, v0, og0 = 3 * cb, 4 * cb, 5 * cb, 6 * cb, 7 * cb

    def spec(rows, c0):
        return pl.BlockSpec((rows, LANES), lambda b, h, c0=c0: (b, c0 + h))

    dmat = jnp.asarray(np.stack([_decay_matrices(False), _decay_matrices(True)]), BF16)
    masks = jnp.asarray(np.stack([_level_masks(False), _level_masks(True)]), F32)
    full = lambda shape: pl.BlockSpec(shape, lambda b, h: (0,) * len(shape))

    return pl.pallas_call(
        _hgrn_kernel,
        grid=(batch, heads),
        in_specs=[spec(s_len, c0) for c0 in (q0, zf0, zb0, v0, og0)]
        + [spec(c_len, c0) for c0 in (q0, zf0, zb0, v0, og0)]
        + [pl.BlockSpec((2, LANES), lambda b, h: (0, h)),
           full((1, HG_DV)), full(dmat.shape), full(masks.shape)],
        out_specs=[pl.BlockSpec((s_len, LANES), lambda b, h: (b, h)),
                   pl.BlockSpec((c_len, LANES), lambda b, h: (b, h))],
        out_shape=[jax.ShapeDtypeStruct((n_lat, d_half), BF16),
                   jax.ShapeDtypeStruct((n_ctx, d_half), BF16)],
        scratch_shapes=[pltpu.VMEM((s_len, HG_DV), F32), pltpu.VMEM((c_len, HG_DV), F32),
                        pltpu.VMEM((s_len, 2 * HG_DK), BF16),
                        pltpu.VMEM((2, s_len // CHUNK, HG_DV, HG_DK), F32),
                        pltpu.VMEM((2, s_len // CHUNK, SUBLANES, HG_DK), F32),
                        pltpu.VMEM((s_len // CHUNK, HG_DV, 2 * HG_DK), BF16)],
        compiler_params=_cparams(("parallel", "parallel")),
        name="hgrn",
    )(*([p_lat] * 5), *([p_ctx] * 5), lb, norm_g.reshape(1, HG_DV), dmat, masks)


def _merge_kernel(cb_ref, cc_ref, cv_ref, ga_ref, gb_ref, yb_ref, x_ref, g1_ref, cw_ref, cbias_ref,
                  wpa_ref, wpb_ref, wo_ref, lng_ref, lnb_ref, o_ref, *, row_len, alpha):
    tm = x_ref.shape[0]
    u = cc_ref[...] * cv_ref[...]
    pos = lax.broadcasted_iota(jnp.int32, (tm, 1), 0) % row_len
    prev = jnp.where(pos == 0, 0.0, pltpu.roll(u, 1, axis=0))
    nxt = jnp.where(pos == row_len - 1, 0.0, pltpu.roll(u, tm - 1, axis=0))
    conv = prev * cw_ref[0:1, :] + u * cw_ref[1:2, :] + nxt * cw_ref[2:3, :] + cbias_ref[...]
    ya = (cb_ref[...] * conv).astype(BF16)
    m = (_sigmoid(ga_ref[...]) * jnp.dot(ya, wpa_ref[...], preferred_element_type=F32)
         + _sigmoid(gb_ref[...]) * jnp.dot(yb_ref[...], wpb_ref[...], preferred_element_type=F32))
    y = jnp.dot(m.astype(BF16), wo_ref[...], preferred_element_type=F32)
    r = alpha * x_ref[...] + g1_ref[0] * y
    o_ref[...] = _ln_rows(r) * lng_ref[...] + lnb_ref[...]


def _merge_call(p, yb, x2d, mods3, conv_w, conv_b, wpa, wpb, wo, ln_g, ln_b, rows_per_mod, mod_row0,
                row_len, tm, alpha):
    n, d_model = x2d.shape
    tm = _tile(rows_per_mod or n, tm)
    d_half = d_model // 2

    if rows_per_mod is None:
        g1_idx = lambda i: (mod_row0 * 6 + 2, 0, 0)
    else:
        g1_idx = lambda i: ((mod_row0 + (i * tm) // rows_per_mod) * 6 + 2, 0, 0)
    const = lambda shape: pl.BlockSpec(shape, lambda i: (0,) * len(shape), pipeline_mode=pl.Buffered(1))
    return pl.pallas_call(
        functools.partial(_merge_kernel, row_len=row_len, alpha=alpha),
        grid=(n // tm,),
        in_specs=[
            pl.BlockSpec((tm, d_half), lambda i: (i, 0)),
            pl.BlockSpec((tm, d_half), lambda i: (i, 1)),
            pl.BlockSpec((tm, d_half), lambda i: (i, 2)),
            pl.BlockSpec((tm, d_model), lambda i: (i, 4)),
            pl.BlockSpec((tm, d_model), lambda i: (i, 5)),
            pl.BlockSpec((tm, d_half), lambda i: (i, 0)),
            pl.BlockSpec((tm, d_model), lambda i: (i, 0)),
            pl.BlockSpec((1, 1, d_model), g1_idx),
            const((3, d_half)), const((1, d_half)),
            const((d_half, d_model)), const((d_half, d_model)), const((d_model, d_model)),
            const((1, d_model)), const((1, d_model)),
        ],
        out_specs=pl.BlockSpec((tm, d_model), lambda i: (i, 0)),
        out_shape=jax.ShapeDtypeStruct((n, d_model), F32),
        compiler_params=_cparams(("parallel",)),
        name="merge",
    )(p, p, p, p, p, yb, x2d, mods3, conv_w, conv_b.reshape(1, d_half), wpa, wpb, wo,
      ln_g.reshape(1, d_model), ln_b.reshape(1, d_model))


def _topk_rows(s, k, payload=None):
    rows = s.shape[0]
    riota = lax.broadcasted_iota(jnp.int32, s.shape, 0).astype(F32)
    vals, picks = [], []
    for _ in range(k):
        m = jnp.max(s, axis=0, keepdims=True)
        am = jnp.min(jnp.where(s == m, riota, float(rows)), axis=0, keepdims=True)
        sel = riota == am
        vals.append(m)
        if payload is None:
            picks.append(am)
        else:
            picks.append(jnp.max(jnp.where(sel, payload, -1.0), axis=0, keepdims=True))
        s = jnp.where(sel, -jnp.inf, s)
    return jnp.concatenate(vals, axis=0), jnp.concatenate(picks, axis=0)


def _pair_candidates():
    return [(a, min(TOPK, TOPK // (a + 1))) for a in range(TOPK)]


def _route_kernel(x_ref, sh_ref, sc_ref, wq_ref, keys_ref, h_ref, ii_ref, jj_ref, gg_ref):
    hmod = _ln_rows(x_ref[...]) * (1.0 + sc_ref[0]) + sh_ref[0]
    h16 = hmod.astype(BF16)
    h_ref[...] = h16
    qp = jnp.dot(h16, wq_ref[...], preferred_element_type=F32).astype(BF16)
    i_rows, j_rows, g_rows = [], [], []
    for head in range(PEER_HEADS):
        sv, si = [], []
        for half in range(2):
            c = (head * 2 + half) * LANES
            s = lax.dot_general(keys_ref[head * 2 + half], qp[:, c:c + LANES], (((1,), (1,)), ((), ())),
                                preferred_element_type=F32)
            v_, i_ = _topk_rows(s, TOPK)
            sv.append(v_)
            si.append(i_)
        pairs = _pair_candidates()
        n_pad = -sum(nb for _, nb in pairs) % SUBLANES
        tm = s.shape[1]
        cand = jnp.concatenate([sv[0][a:a + 1, :] + sv[1][0:nb, :] for a, nb in pairs]
                               + [jnp.full((n_pad, tm), -jnp.inf, F32)], axis=0)
        cidx = jnp.concatenate([si[0][a:a + 1, :] * float(N_KEYS) + si[1][0:nb, :] for a, nb in pairs]
                               + [jnp.zeros((n_pad, tm), F32)], axis=0)
        tv, te = _topk_rows(cand, TOPK, payload=cidx)
        te = te.astype(jnp.int32)
        ex = jnp.exp(tv - tv[0:1, :])
        g_rows.append(ex / jnp.sum(ex, axis=0, keepdims=True))
        i_rows.append(lax.shift_right_logical(te, 7))
        j_rows.append(te & (N_KEYS - 1))
    ii_ref[...] = jnp.concatenate(i_rows, axis=0).T
    jj_ref[...] = jnp.concatenate(j_rows, axis=0).T
    gg_ref[...] = jnp.concatenate(g_rows, axis=0).T


def _route_call(x2d, mods3, wq, keys, rows_per_mod, mod_row0, tm):
    n, d_model = x2d.shape
    tm = _tile(rows_per_mod or n, tm)
    slots = PEER_HEADS * TOPK

    def mod_idx(g):
        if rows_per_mod is None:
            return lambda i: (mod_row0 * 6 + g, 0, 0)
        return lambda i: ((mod_row0 + (i * tm) // rows_per_mod) * 6 + g, 0, 0)

    const = lambda shape: pl.BlockSpec(shape, lambda i: (0,) * len(shape), pipeline_mode=pl.Buffered(1))
    return pl.pallas_call(
        _route_kernel,
        grid=(n // tm,),
        in_specs=[
            pl.BlockSpec((tm, d_model), lambda i: (i, 0)),
            pl.BlockSpec((1, 1, d_model), mod_idx(3)),
            pl.BlockSpec((1, 1, d_model), mod_idx(4)),
            const(wq.shape), const(keys.shape),
        ],
        out_specs=[pl.BlockSpec((tm, d_model), lambda i: (i, 0))]
        + [pl.BlockSpec((tm, slots), lambda i: (i, 0))] * 3,
        out_shape=[jax.ShapeDtypeStruct((n, d_model), BF16),
                   jax.ShapeDtypeStruct((n, slots), jnp.int32),
                   jax.ShapeDtypeStruct((n, slots), jnp.int32),
                   jax.ShapeDtypeStruct((n, slots), F32)],
        compiler_params=_cparams(("parallel",)),
        name="route",
    )(x2d, mods3, mods3, wq, keys)


def _gelu_exact(x):
    return 0.5 * x * (1.0 + lax.erf(x * (2.0 ** -0.5)))


def _gates_kernel(ii_ref, jj_ref, gg_ref, o_ref, gate_s):
    tm = ii_ref.shape[0]
    sub = lax.broadcasted_iota(jnp.int32, (N_KEYS, N_KEYS), 0)

    def token(n, carry):
        ii = ii_ref[pl.ds(n, 1), :]
        jj = jj_ref[pl.ds(n, 1), :]
        gg = gg_ref[pl.ds(n, 1), :]
        a_t = jnp.where(ii == sub, gg, 0.0).astype(BF16)
        b_t = jnp.where(jj == sub, 1.0, 0.0).astype(BF16)
        gate = lax.dot_general(a_t, b_t, (((1,), (1,)), ((), ())), preferred_element_type=F32)
        gate_s[pl.ds(pl.multiple_of(n * G_PITCH, SUBLANES), N_KEYS), :] = gate
        return carry

    lax.fori_loop(0, tm, token, 0, unroll=16)
    for i in range(N_KEYS):
        o_ref[:, i * N_KEYS:(i + 1) * N_KEYS] = gate_s[pl.ds(i, tm, stride=G_PITCH), :].astype(o_ref.dtype)


def _gates_call(ii, jj, gg, tm):
    n, slots = ii.shape
    tm = _tile(n, tm)
    tok = pl.BlockSpec((tm, slots), lambda i: (i, 0))
    return pl.pallas_call(
        _gates_kernel,
        grid=(n // tm,),
        in_specs=[tok, tok, tok],
        out_specs=pl.BlockSpec((tm, N_KEYS * N_KEYS), lambda i: (i, 0)),
        out_shape=jax.ShapeDtypeStruct((n, N_KEYS * N_KEYS), BF16),
        scratch_shapes=[pltpu.VMEM((tm * G_PITCH, LANES), F32)],
        compiler_params=_cparams(("parallel",)),
        name="gates",
    )(ii, jj, gg)


def _dense_kernel(h_ref, gate_ref, ut_ref, v_ref, x_ref, g2_ref, lng_ref, lnb_ref, o_ref, *, alpha):
    eb = pl.program_id(1)

    @pl.when(eb == 0)
    def _():
        o_ref[...] = jnp.zeros(o_ref.shape, F32)

    act = _gelu_exact(jnp.dot(h_ref[...], ut_ref[0, 0], preferred_element_type=F32))
    hg = (gate_ref[...].astype(F32) * act).astype(BF16)
    o_ref[...] += jnp.dot(hg, v_ref[0], preferred_element_type=F32)

    @pl.when(eb == pl.num_programs(1) - 1)
    def _():
        r = alpha * x_ref[...] + g2_ref[0] * o_ref[...]
        o_ref[...] = _ln_rows(r) * lng_ref[...] + lnb_ref[...]


def _dense_call(h16, gates, ut4, v16, layer, x2d, mods3, ln_g, ln_b, rows_per_mod, mod_row0, tm, alpha):
    n, d_model = x2d.shape
    tm = _tile(rows_per_mod or n, tm)
    _, n_blocks, _, te = ut4.shape

    if rows_per_mod is None:
        g2_idx = lambda i, e: (mod_row0 * 6 + 5, 0, 0)
    else:
        g2_idx = lambda i, e: ((mod_row0 + (i * tm) // rows_per_mod) * 6 + 5, 0, 0)
    tok = lambda cols: pl.BlockSpec((tm, cols), lambda i, e: (i, 0))
    const = lambda shape: pl.BlockSpec(shape, lambda i, e: (0,) * len(shape))
    return pl.pallas_call(
        functools.partial(_dense_kernel, alpha=alpha),
        grid=(n // tm, n_blocks),
        in_specs=[
            tok(d_model),
            pl.BlockSpec((tm, te), lambda i, e: (i, e)),
            pl.BlockSpec((1, 1, d_model, te), lambda i, e: (layer, e, 0, 0)),
            pl.BlockSpec((1, te, d_model), lambda i, e: (layer, e, 0)),
            tok(d_model),
            pl.BlockSpec((1, 1, d_model), g2_idx),
            const((1, d_model)), const((1, d_model)),
        ],
        out_specs=tok(d_model),
        out_shape=jax.ShapeDtypeStruct((n, d_model), F32),
        compiler_params=_cparams(("parallel", "arbitrary")),
        name="dense",
    )(h16, gates, ut4, v16, x2d, mods3, ln_g.reshape(1, d_model), ln_b.reshape(1, d_model))


def kernel(x, c, ctx, c_ctx, w_mod, b_mod, w_in, conv_w, conv_b, lb_raw, hg_norm_g, w_pa, w_pb, w_o,
           ln1_g, ln1_b, peer_wq, peer_keys, peer_u, peer_v, ln2_g, ln2_b):
    batch, s_len, d_model = x.shape
    c_len = ctx.shape[1]
    depth = w_mod.shape[0]
    alpha = (2.0 * depth) ** 0.25
    n_lat = batch * s_len
    n_ctx = batch * c_len

    p = jax.nn.softmax(lb_raw.astype(F32), axis=0)
    lower = jnp.cumsum(p, axis=0) - p[:1]

    cond = jnp.concatenate([c, c_ctx[None, :]], axis=0)
    cb = jnp.broadcast_to(cond.T[:, :, None], (d_model, batch + 1, LANES)).reshape(d_model, -1)
    mods = _mod_call(cb, w_mod, b_mod, batch + 1)

    w_in16 = _cast_call(w_in)
    v16 = _cast_call(peer_v)
    ut4 = _cast_t_call(peer_u, DENSE_TE)

    xl = x.reshape(n_lat, d_model)
    xc = ctx.reshape(n_ctx, d_model)
    for l in range(depth):
        last = l == depth - 1
        mods3 = mods[l].reshape(16 * 6, 1, d_model)
        wpa, wpb, wo = w_pa[l].astype(BF16), w_pb[l].astype(BF16), w_o[l].astype(BF16)
        wq = peer_wq[l].astype(BF16)
        keys = peer_keys[l].reshape(PEER_HEADS * 2, N_KEYS, -1).astype(BF16)

        p_lat = _inproj_call(xl, mods3, w_in16, l, s_len, 0, 1024)
        p_ctx = _inproj_call(xc, mods3, w_in16, l, None, batch, 1024)
        yb_lat, yb_ctx = _hgrn_call(p_lat, p_ctx, lower[l], hg_norm_g[l], batch, d_model)

        streams = [(xl, p_lat, yb_lat, s_len, 0, GRID_W)]
        if not last:
            streams.append((xc, p_ctx, yb_ctx, None, batch, c_len))
        outs = []
        for xs, ps, ybs, rpm, row0, row_len in streams:
            x1 = _merge_call(ps, ybs, xs, mods3, conv_w[l], conv_b[l], wpa, wpb, wo, ln1_g[l], ln1_b[l],
                             rpm, row0, row_len, 256, alpha)
            h16, ii, jj, gg = _route_call(x1, mods3, wq, keys, rpm, row0, 256)
            gates = _gates_call(ii, jj, gg, 256)
            outs.append(_dense_call(h16, gates, ut4, v16, l, x1, mods3, ln2_g[l], ln2_b[l],
                                    rpm, row0, DENSE_TM, alpha))
        xl = outs[0]
        if not last:
            xc = outs[1]
    return xl.reshape(batch, s_len, d_model)
```

```python
import functools
import math

import numpy as np
import jax
import jax.numpy as jnp
from jax import lax
from jax.experimental import pallas as pl
from jax.experimental.pallas import tpu as pltpu

F32 = jnp.float32
BF16 = jnp.bfloat16

LANES = 128
SUBLANES = 8
VMEM_LIMIT = 56 * 1024 * 1024

EPS = 1e-6
F_MIN = 1e-30
GRID_W = 64
HG_DK = 128
HG_DV = 128
CHUNK = 64
LEVELS = (32, 16, 8, 4, 2, 1, 0)
MATMUL_LEVELS = (4, 2)
INTRA_CHUNKS = 4
N_KEYS = 128
TOPK = 16
PEER_HEADS = 8
G_PITCH = 132
GATE_UNROLL = 16
DENSE_TE = 1024
DENSE_TM = 512
CAST_BLOCK_BYTES = 4 * 1024 * 1024


def _cparams(sem):
    return pltpu.CompilerParams(dimension_semantics=sem, vmem_limit_bytes=VMEM_LIMIT)


def _tile(n, pref):
    t = min(n, pref)
    while n % t:
        t //= 2
    return t


def _ln_rows(x):
    mu = jnp.mean(x, axis=-1, keepdims=True)
    xc = x - mu
    var = jnp.mean(xc * xc, axis=-1, keepdims=True)
    return xc * lax.rsqrt(var + EPS)


def _sigmoid(x):
    return 1.0 / (1.0 + jnp.exp(-x))


def _silu(x):
    return x * _sigmoid(x)


def _split_bf16(x):
    hi = x.astype(BF16)
    lo = (x - hi.astype(F32)).astype(BF16)
    return hi, lo


def _cast_kernel(x_ref, o_ref):
    o_ref[...] = x_ref[...].astype(o_ref.dtype)


def _cast_call(w):
    depth, r, c = w.shape
    rows = _tile(r, 1 << max(4, (CAST_BLOCK_BYTES // (4 * c)).bit_length() - 1))
    spec = pl.BlockSpec((1, rows, c), lambda l, i: (l, i, 0))
    return pl.pallas_call(
        _cast_kernel, grid=(depth, r // rows), in_specs=[spec], out_specs=spec,
        out_shape=jax.ShapeDtypeStruct(w.shape, BF16),
        compiler_params=_cparams(("parallel", "parallel")), name="cast",
    )(w)


def _cast_t_kernel(x_ref, o_ref):
    d = x_ref.shape[2]
    step = 256 if d % 256 == 0 else d
    for c in range(0, d, step):
        o_ref[0, 0, c:c + step, :] = x_ref[0, :, c:c + step].T.astype(o_ref.dtype)


def _cast_t_call(w, te):
    depth, e, d = w.shape
    return pl.pallas_call(
        _cast_t_kernel, grid=(depth, e // te),
        in_specs=[pl.BlockSpec((1, te, d), lambda l, i: (l, i, 0))],
        out_specs=pl.BlockSpec((1, 1, d, te), lambda l, i: (l, i, 0, 0)),
        out_shape=jax.ShapeDtypeStruct((depth, e // te, d, te), BF16),
        compiler_params=_cparams(("parallel", "parallel")), name="cast_t",
    )(w)


def _mod_kernel(cb_ref, w_ref, b_ref, o_ref, s_tab, *, n_rows):
    d_model = w_ref.shape[1]
    tn = w_ref.shape[2]

    @pl.when((pl.program_id(0) == 0) & (pl.program_id(1) == 0))
    def _():
        def fill(g, carry):
            r0 = pl.multiple_of(g * 64, 64)
            s_tab[pl.ds(r0, 64), :] = _silu(cb_ref[pl.ds(r0, 64), :])
            return carry

        lax.fori_loop(0, d_model // 64, fill, 0)

    def body(g, accs):
        r0 = pl.multiple_of(g * SUBLANES, SUBLANES)
        w8 = w_ref[0, pl.ds(r0, SUBLANES), :]
        out = []
        for r, acc in enumerate(accs):
            s8 = s_tab[pl.ds(r0, SUBLANES), r * LANES:(r + 1) * LANES]
            out.append(acc + jnp.concatenate([s8] * (tn // LANES), axis=1) * w8)
        return tuple(out)

    init = tuple(jnp.zeros((SUBLANES, tn), F32) for _ in range(n_rows))
    accs = lax.fori_loop(0, d_model // SUBLANES, body, init, unroll=2)
    bias = b_ref[0]
    o_ref[0] = jnp.zeros((o_ref.shape[1], tn), F32)
    for r, acc in enumerate(accs):
        o_ref[0, r:r + 1, :] = jnp.sum(acc, axis=0, keepdims=True) + bias


def _mod_call(cb, w_mod, b_mod, n_rows):
    depth, d_model, width = w_mod.shape
    tn = 512
    return pl.pallas_call(
        functools.partial(_mod_kernel, n_rows=n_rows),
        grid=(depth, width // tn),
        in_specs=[
            pl.BlockSpec((d_model, n_rows * LANES), lambda l, j: (0, 0), pipeline_mode=pl.Buffered(1)),
            pl.BlockSpec((1, d_model, tn), lambda l, j: (l, 0, j)),
            pl.BlockSpec((1, 1, tn), lambda l, j: (l, 0, j)),
        ],
        out_specs=pl.BlockSpec((1, 16, tn), lambda l, j: (l, 0, j)),
        out_shape=jax.ShapeDtypeStruct((depth, 16, width), F32),
        scratch_shapes=[pltpu.VMEM((d_model, n_rows * LANES), F32)],
        compiler_params=_cparams(("arbitrary", "arbitrary")),
        name="mod",
    )(cb, w_mod, b_mod.reshape(depth, 1, width))


def _inproj_kernel(x_ref, sh_ref, sc_ref, w_ref, o_ref, h_ref):
    @pl.when(pl.program_id(1) == 0)
    def _():
        y = _ln_rows(x_ref[...])
        h_ref[...] = (y * (1.0 + sc_ref[0]) + sh_ref[0]).astype(BF16)

    o_ref[...] = jnp.dot(h_ref[...], w_ref[0], preferred_element_type=F32)


def _inproj_call(x2d, mods3, w_bf16, layer, rows_per_mod, mod_row0, tm):
    n, d_model = x2d.shape
    tm = _tile(rows_per_mod or n, tm)
    width = w_bf16.shape[2]
    tn = _tile(width, 1024)

    def mod_idx(g):
        if rows_per_mod is None:
            return lambda i, j: (mod_row0 * 6 + g, 0, 0)
        return lambda i, j: ((mod_row0 + (i * tm) // rows_per_mod) * 6 + g, 0, 0)

    return pl.pallas_call(
        _inproj_kernel,
        grid=(n // tm, width // tn),
        in_specs=[
            pl.BlockSpec((tm, d_model), lambda i, j: (i, 0)),
            pl.BlockSpec((1, 1, d_model), mod_idx(0)),
            pl.BlockSpec((1, 1, d_model), mod_idx(1)),
            pl.BlockSpec((1, d_model, tn), lambda i, j: (layer, 0, j)),
        ],
        out_specs=pl.BlockSpec((tm, tn), lambda i, j: (i, j)),
        out_shape=jax.ShapeDtypeStruct((n, width), F32),
        scratch_shapes=[pltpu.VMEM((tm, d_model), BF16)],
        compiler_params=_cparams(("parallel", "arbitrary")),
        name="inproj",
    )(x2d, mods3, mods3, w_bf16)


def _decay_matrices(reverse):
    L = CHUNK
    rank = np.arange(L)[::-1] if reverse else np.arange(L)
    rt = rank[:, None]
    rr = rank[None, :]
    blocks = [(rr <= rt)]
    for h in MATMUL_LEVELS:
        upper = (rt & h) != 0
        blocks.append(upper & (rr >= (rt & ~(h - 1))) & (rr <= rt))
        blocks.append((~upper) & (rr > rt) & (rr <= (rt | (h - 1))))
    return np.concatenate(blocks, axis=0).astype(np.float32)


def _half_block_exponents(bcum, h, reverse):
    eq, ek = [], []
    zero = jnp.zeros((h, LANES), F32)
    for b in range(CHUNK // h):
        rows = bcum[b * h:(b + 1) * h]
        first_of_pair = b % 2 == 0
        if not reverse:
            if first_of_pair:
                eq.append(zero)
                ek.append(bcum[(b + 1) * h - 1:(b + 1) * h] - rows)
            else:
                eq.append(rows - bcum[b * h - 1:b * h])
                ek.append(zero)
        else:
            if first_of_pair:
                eq.append(rows - bcum[(b + 1) * h:(b + 1) * h + 1])
                ek.append(zero)
            else:
                eq.append(zero)
                ek.append(bcum[b * h:b * h + 1] - rows)
    return jnp.concatenate(eq, axis=0), jnp.concatenate(ek, axis=0)


def _level_masks(reverse):
    L = CHUNK
    rank = np.arange(L)[::-1] if reverse else np.arange(L)
    rt = rank[:, None]
    rs = rank[None, :]
    out = []
    for h in LEVELS:
        if h == 0:
            out.append(rt == rs)
        else:
            out.append(((rt ^ rs) // h == 1) & (rt > rs))
    return np.stack(out).astype(np.float32)


def _hgrn_intra(chains):
    L = CHUNK
    gs, ks, es, bcums, blasts = [], [], [], [], []
    for qs, z, v16, lb, dmat, masks, reverse in chains:
        sig = _sigmoid(z)
        f = lb + (1.0 - lb) * sig
        g = jnp.log(jnp.maximum(f, F_MIN))
        g_hi, g_lo = _split_bf16(g)
        e2 = jnp.dot(dmat, jnp.concatenate([g_hi, g_lo], axis=1), preferred_element_type=F32)
        e = e2[:, :LANES] + e2[:, LANES:]
        gs.append(g)
        ks.append((1.0 - lb) * (1.0 - sig))
        es.append(e)
        bcums.append(e[0:L])
        blasts.append(e[0:1] if reverse else e[L - 1:L])

    scores = [jnp.zeros((L, L), F32) for _ in chains]
    for i, h in enumerate(LEVELS):
        for c, (qs, z, v16, lb, dmat, masks, reverse) in enumerate(chains):
            k = ks[c]
            if h <= 1:
                if h == 0:
                    prod = qs * k
                else:
                    prod = qs * jnp.exp(gs[c]) * pltpu.roll(k, L - 1 if reverse else 1, axis=0)
                scores[c] = scores[c] + masks[i] * jnp.sum(prod, axis=-1, keepdims=True)
                continue
            if h in MATMUL_LEVELS:
                m = 1 + 2 * MATMUL_LEVELS.index(h)
                qt = qs * jnp.exp(es[c][m * L:(m + 1) * L])
                kt = k * jnp.exp(es[c][(m + 1) * L:(m + 2) * L])
            else:
                eq, ek = _half_block_exponents(bcums[c], h, reverse)
                qt = qs * jnp.exp(eq)
                kt = k * jnp.exp(ek)
            s_h = lax.dot_general(qt.astype(BF16), kt.astype(BF16), (((1,), (1,)), ((), ())),
                                  preferred_element_type=F32)
            scores[c] = scores[c] + masks[i] * s_h

    outs = []
    for c, (qs, z, v16, lb, dmat, masks, reverse) in enumerate(chains):
        o = jnp.dot(scores[c].astype(BF16), v16, preferred_element_type=F32)
        qhat = (qs * jnp.exp(bcums[c])).astype(BF16)
        kvt = lax.dot_general(v16, (ks[c] * jnp.exp(blasts[c] - bcums[c])).astype(BF16),
                              (((0,), (0,)), ((), ())), preferred_element_type=F32)
        outs.append((o, qhat, kvt, blasts[c]))
    return outs


def _hgrn_kernel(ql, zfl, zbl, vl, ogl, qc, zfc, zbc, vc, ogc, lb_ref, ng_ref, dm_ref, mk_ref,
                 yl_ref, yc_ref, ol_s, oc_s, qh_s, kv_s, bl_s, st_s):
    lbf = lb_ref[0:1, :]
    lbb = lb_ref[1:2, :]

    def run(q_ref, zf_ref, zb_ref, v_ref, o_s, states):
        n = q_ref.shape[0] // CHUNK
        per_it = math.gcd(n, INTRA_CHUNKS)

        def intra(it, carry):
            chains = []
            for u in range(per_it):
                c = it * per_it + u
                r = pl.multiple_of(c * CHUNK, CHUNK)
                qs = _silu(q_ref[pl.ds(r, CHUNK), :]) * (HG_DK ** -0.5)
                v16 = v_ref[pl.ds(r, CHUNK), :].astype(BF16)
                chains.append((qs, zf_ref[pl.ds(r, CHUNK), :], v16, lbf, dm_ref[0], mk_ref[0], False))
                chains.append((qs, zb_ref[pl.ds(r, CHUNK), :], v16, lbb, dm_ref[1], mk_ref[1], True))
            res = _hgrn_intra(chains)
            for u in range(per_it):
                c = it * per_it + u
                r = pl.multiple_of(c * CHUNK, CHUNK)
                (of, qf, kvf, blf), (ob, qb, kvb, blb) = res[2 * u], res[2 * u + 1]
                o_s[pl.ds(r, CHUNK), :] = of + ob
                qh_s[pl.ds(r, CHUNK), 0:HG_DK] = qf
                qh_s[pl.ds(r, CHUNK), HG_DK:2 * HG_DK] = qb
                kv_s[0, c] = kvf
                kv_s[1, c] = kvb
                bl_s[0, c] = jnp.broadcast_to(blf, (SUBLANES, HG_DK))
                bl_s[1, c] = jnp.broadcast_to(blb, (SUBLANES, HG_DK))
            return carry

        lax.fori_loop(0, n // per_it, intra, 0)

        def recur(c, carry):
            sf, sb = carry
            cb = n - 1 - c
            st_s[c, :, 0:HG_DK] = sf.astype(BF16)
            st_s[cb, :, HG_DK:2 * HG_DK] = sb.astype(BF16)
            sf = sf * jnp.exp(bl_s[0, c][0:1, :]) + kv_s[0, c]
            sb = sb * jnp.exp(bl_s[1, cb][0:1, :]) + kv_s[1, cb]
            return sf, sb

        states = lax.fori_loop(0, n, recur, states)

        def inter(c, carry):
            r = pl.multiple_of(c * CHUNK, CHUNK)
            o_s[pl.ds(r, CHUNK), :] += lax.dot_general(
                qh_s[pl.ds(r, CHUNK), :], st_s[c], (((1,), (1,)), ((), ())), preferred_element_type=F32)
            return carry

        lax.fori_loop(0, n, inter, 0, unroll=4)
        return states

    zero = jnp.zeros((HG_DV, HG_DK), F32)
    states = run(qc, zfc, zbc, vc, oc_s, (zero, zero))
    run(ql, zfl, zbl, vl, ol_s, states)

    def readout(o_s, og_ref, y_ref):
        o = o_s[...]
        o = o * lax.rsqrt(jnp.mean(o * o, axis=-1, keepdims=True) + EPS) * ng_ref[...]
        y_ref[...] = (o * _silu(og_ref[...])).astype(y_ref.dtype)

    readout(ol_s, ogl, yl_ref)
    readout(oc_s, ogc, yc_ref)


def _hgrn_call(p_lat, p_ctx, lb, norm_g, batch, d_model):
    n_lat = p_lat.shape[0]
    n_ctx = p_ctx.shape[0]
    s_len = n_lat // batch
    c_len = n_ctx // batch
    d_half = d_model // 2
    heads = d_half // HG_DK
    cb = d_half // LANES
    q0, zf0, zb0, v0, og0 = 3 * cb, 4 * cb, 5 * cb, 6 * cb, 7 * cb

    def spec(rows, c0):
        return pl.BlockSpec((rows, LANES), lambda b, h, c0=c0: (b, c0 + h))

    dmat = jnp.asarray(np.stack([_decay_matrices(False), _decay_matrices(True)]), BF16)
    masks = jnp.asarray(np.stack([_level_masks(False), _level_masks(True)]), F32)
    full = lambda shape: pl.BlockSpec(shape, lambda b, h: (0,) * len(shape))

    return pl.pallas_call(
        _hgrn_kernel,
        grid=(batch, heads),
        in_specs=[spec(s_len, c0) for c0 in (q0, zf0, zb0, v0, og0)]
        + [spec(c_len, c0) for c0 in (q0, zf0, zb0, v0, og0)]
        + [pl.BlockSpec((2, LANES), lambda b, h: (0, h)),
           full((1, HG_DV)), full(dmat.shape), full(masks.shape)],
        out_specs=[pl.BlockSpec((s_len, LANES), lambda b, h: (b, h)),
                   pl.BlockSpec((c_len, LANES), lambda b, h: (b, h))],
        out_shape=[jax.ShapeDtypeStruct((n_lat, d_half), BF16),
                   jax.ShapeDtypeStruct((n_ctx, d_half), BF16)],
        scratch_shapes=[pltpu.VMEM((s_len, HG_DV), F32), pltpu.VMEM((c_len, HG_DV), F32),
                        pltpu.VMEM((s_len, 2 * HG_DK), BF16),
                        pltpu.VMEM((2, s_len // CHUNK, HG_DV, HG_DK), F32),
                        pltpu.VMEM((2, s_len // CHUNK, SUBLANES, HG_DK), F32),
                        pltpu.VMEM((s_len // CHUNK, HG_DV, 2 * HG_DK), BF16)],
        compiler_params=_cparams(("parallel", "parallel")),
        name="hgrn",
    )(*([p_lat] * 5), *([p_ctx] * 5), lb, norm_g.reshape(1, HG_DV), dmat, masks)


def _merge_kernel(cb_ref, cc_ref, cv_ref, ga_ref, gb_ref, yb_ref, x_ref, g1_ref, cw_ref, cbias_ref,
                  wpa_ref, wpb_ref, wo_ref, lng_ref, lnb_ref, o_ref, *, row_len, alpha):
    tm = x_ref.shape[0]
    u = cc_ref[...] * cv_ref[...]
    pos = lax.broadcasted_iota(jnp.int32, (tm, 1), 0) % row_len
    prev = jnp.where(pos == 0, 0.0, pltpu.roll(u, 1, axis=0))
    nxt = jnp.where(pos == row_len - 1, 0.0, pltpu.roll(u, tm - 1, axis=0))
    conv = prev * cw_ref[0:1, :] + u * cw_ref[1:2, :] + nxt * cw_ref[2:3, :] + cbias_ref[...]
    ya = (cb_ref[...] * conv).astype(BF16)
    m = (_sigmoid(ga_ref[...]) * jnp.dot(ya, wpa_ref[...], preferred_element_type=F32)
         + _sigmoid(gb_ref[...]) * jnp.dot(yb_ref[...], wpb_ref[...], preferred_element_type=F32))
    y = jnp.dot(m.astype(BF16), wo_ref[...], preferred_element_type=F32)
    r = alpha * x_ref[...] + g1_ref[0] * y
    o_ref[...] = _ln_rows(r) * lng_ref[...] + lnb_ref[...]


def _merge_call(p, yb, x2d, mods3, conv_w, conv_b, wpa, wpb, wo, ln_g, ln_b, rows_per_mod, mod_row0,
                row_len, tm, alpha):
    n, d_model = x2d.shape
    tm = _tile(rows_per_mod or n, tm)
    d_half = d_model // 2

    if rows_per_mod is None:
        g1_idx = lambda i: (mod_row0 * 6 + 2, 0, 0)
    else:
        g1_idx = lambda i: ((mod_row0 + (i * tm) // rows_per_mod) * 6 + 2, 0, 0)
    const = lambda shape: pl.BlockSpec(shape, lambda i: (0,) * len(shape), pipeline_mode=pl.Buffered(1))
    return pl.pallas_call(
        functools.partial(_merge_kernel, row_len=row_len, alpha=alpha),
        grid=(n // tm,),
        in_specs=[
            pl.BlockSpec((tm, d_half), lambda i: (i, 0)),
            pl.BlockSpec((tm, d_half), lambda i: (i, 1)),
            pl.BlockSpec((tm, d_half), lambda i: (i, 2)),
            pl.BlockSpec((tm, d_model), lambda i: (i, 4)),
            pl.BlockSpec((tm, d_model), lambda i: (i, 5)),
            pl.BlockSpec((tm, d_half), lambda i: (i, 0)),
            pl.BlockSpec((tm, d_model), lambda i: (i, 0)),
            pl.BlockSpec((1, 1, d_model), g1_idx),
            const((3, d_half)), const((1, d_half)),
            const((d_half, d_model)), const((d_half, d_model)), const((d_model, d_model)),
            const((1, d_model)), const((1, d_model)),
        ],
        out_specs=pl.BlockSpec((tm, d_model), lambda i: (i, 0)),
        out_shape=jax.ShapeDtypeStruct((n, d_model), F32),
        compiler_params=_cparams(("parallel",)),
        name="merge",
    )(p, p, p, p, p, yb, x2d, mods3, conv_w, conv_b.reshape(1, d_half), wpa, wpb, wo,
      ln_g.reshape(1, d_model), ln_b.reshape(1, d_model))


def _topk_rows(s, k, payload=None):
    rows = s.shape[0]
    riota = lax.broadcasted_iota(jnp.int32, s.shape, 0).astype(F32)
    vals, picks = [], []
    for _ in range(k):
        m = jnp.max(s, axis=0, keepdims=True)
        am = jnp.min(jnp.where(s == m, riota, float(rows)), axis=0, keepdims=True)
        sel = riota == am
        vals.append(m)
        if payload is None:
            picks.append(am)
        else:
            picks.append(jnp.max(jnp.where(sel, payload, -1.0), axis=0, keepdims=True))
        s = jnp.where(sel, -jnp.inf, s)
    return jnp.concatenate(vals, axis=0), jnp.concatenate(picks, axis=0)


def _pair_candidates():
    return [(a, min(TOPK, TOPK // (a + 1))) for a in range(TOPK)]


def _route_kernel(x_ref, sh_ref, sc_ref, wq_ref, keys_ref, h_ref, ii_ref, jj_ref, gg_ref):
    hmod = _ln_rows(x_ref[...]) * (1.0 + sc_ref[0]) + sh_ref[0]
    h16 = hmod.astype(BF16)
    h_ref[...] = h16
    qp = jnp.dot(h16, wq_ref[...], preferred_element_type=F32).astype(BF16)
    i_rows, j_rows, g_rows = [], [], []
    for head in range(PEER_HEADS):
        sv, si = [], []
        for half in range(2):
            c = (head * 2 + half) * LANES
            s = lax.dot_general(keys_ref[head * 2 + half], qp[:, c:c + LANES], (((1,), (1,)), ((), ())),
                                preferred_element_type=F32)
            v_, i_ = _topk_rows(s, TOPK)
            sv.append(v_)
            si.append(i_)
        pairs = _pair_candidates()
        n_pad = -sum(nb for _, nb in pairs) % SUBLANES
        tm = s.shape[1]
        cand = jnp.concatenate([sv[0][a:a + 1, :] + sv[1][0:nb, :] for a, nb in pairs]
                               + [jnp.full((n_pad, tm), -jnp.inf, F32)], axis=0)
        cidx = jnp.concatenate([si[0][a:a + 1, :] * float(N_KEYS) + si[1][0:nb, :] for a, nb in pairs]
                               + [jnp.zeros((n_pad, tm), F32)], axis=0)
        tv, te = _topk_rows(cand, TOPK, payload=cidx)
        te = te.astype(jnp.int32)
        ex = jnp.exp(tv - tv[0:1, :])
        g_rows.append(ex / jnp.sum(ex, axis=0, keepdims=True))
        i_rows.append(lax.shift_right_logical(te, 7))
        j_rows.append(te & (N_KEYS - 1))
    ii_ref[...] = jnp.concatenate(i_rows, axis=0).T
    jj_ref[...] = jnp.concatenate(j_rows, axis=0).T
    gg_ref[...] = jnp.concatenate(g_rows, axis=0).T


def _route_call(x2d, mods3, wq, keys, rows_per_mod, mod_row0, tm):
    n, d_model = x2d.shape
    tm = _tile(rows_per_mod or n, tm)
    slots = PEER_HEADS * TOPK

    def mod_idx(g):
        if rows_per_mod is None:
            return lambda i: (mod_row0 * 6 + g, 0, 0)
        return lambda i: ((mod_row0 + (i * tm) // rows_per_mod) * 6 + g, 0, 0)

    const = lambda shape: pl.BlockSpec(shape, lambda i: (0,) * len(shape), pipeline_mode=pl.Buffered(1))
    return pl.pallas_call(
        _route_kernel,
        grid=(n // tm,),
        in_specs=[
            pl.BlockSpec((tm, d_model), lambda i: (i, 0)),
            pl.BlockSpec((1, 1, d_model), mod_idx(3)),
            pl.BlockSpec((1, 1, d_model), mod_idx(4)),
            const(wq.shape), const(keys.shape),
        ],
        out_specs=[pl.BlockSpec((tm, d_model), lambda i: (i, 0))]
        + [pl.BlockSpec((tm, slots), lambda i: (i, 0))] * 3,
        out_shape=[jax.ShapeDtypeStruct((n, d_model), BF16),
                   jax.ShapeDtypeStruct((n, slots), jnp.int32),
                   jax.ShapeDtypeStruct((n, slots), jnp.int32),
                   jax.ShapeDtypeStruct((n, slots), F32)],
        compiler_params=_cparams(("parallel",)),
        name="route",
    )(x2d, mods3, mods3, wq, keys)


def _gelu_exact(x):
    return 0.5 * x * (1.0 + lax.erf(x * (2.0 ** -0.5)))


def _gates_kernel(ii_ref, jj_ref, gg_ref, o_ref, gate_s):
    tm = ii_ref.shape[0]
    sub = lax.broadcasted_iota(jnp.int32, (N_KEYS, N_KEYS), 0)

    def tokens(it, carry):
        base = pl.multiple_of(it * (GATE_UNROLL * G_PITCH), SUBLANES)
        for u in range(GATE_UNROLL):
            n = it * GATE_UNROLL + u
            ii = ii_ref[pl.ds(n, 1), :]
            jj = jj_ref[pl.ds(n, 1), :]
            gg = gg_ref[pl.ds(n, 1), :]
            a_t = jnp.where(ii == sub, gg, 0.0).astype(BF16)
            b_t = jnp.where(jj == sub, 1.0, 0.0).astype(BF16)
            gate = lax.dot_general(a_t, b_t, (((1,), (1,)), ((), ())), preferred_element_type=F32)
            gate_s[pl.ds(base + u * G_PITCH, N_KEYS), :] = gate
        return carry

    lax.fori_loop(0, tm // GATE_UNROLL, tokens, 0)
    for i in range(N_KEYS):
        o_ref[:, i * N_KEYS:(i + 1) * N_KEYS] = gate_s[pl.ds(i, tm, stride=G_PITCH), :].astype(o_ref.dtype)


def _gates_call(ii, jj, gg, tm):
    n, slots = ii.shape
    tm = _tile(n, tm)
    tok = pl.BlockSpec((tm, slots), lambda i: (i, 0))
    return pl.pallas_call(
        _gates_kernel,
        grid=(n // tm,),
        in_specs=[tok, tok, tok],
        out_specs=pl.BlockSpec((tm, N_KEYS * N_KEYS), lambda i: (i, 0)),
        out_shape=jax.ShapeDtypeStruct((n, N_KEYS * N_KEYS), BF16),
        scratch_shapes=[pltpu.VMEM((tm * G_PITCH, LANES), F32)],
        compiler_params=_cparams(("parallel",)),
        name="gates",
    )(ii, jj, gg)


def _dense_kernel(h_ref, gate_ref, ut_ref, v_ref, x_ref, g2_ref, lng_ref, lnb_ref, o_ref, *, alpha):
    eb = pl.program_id(1)

    @pl.when(eb == 0)
    def _():
        o_ref[...] = jnp.zeros(o_ref.shape, F32)

    act = _gelu_exact(jnp.dot(h_ref[...], ut_ref[0, 0], preferred_element_type=F32))
    hg = (gate_ref[...].astype(F32) * act).astype(BF16)
    o_ref[...] += jnp.dot(hg, v_ref[0], preferred_element_type=F32)

    @pl.when(eb == pl.num_programs(1) - 1)
    def _():
        r = alpha * x_ref[...] + g2_ref[0] * o_ref[...]
        o_ref[...] = _ln_rows(r) * lng_ref[...] + lnb_ref[...]


def _dense_call(h16, gates, ut4, v16, layer, x2d, mods3, ln_g, ln_b, rows_per_mod, mod_row0, tm, alpha):
    n, d_model = x2d.shape
    tm = _tile(rows_per_mod or n, tm)
    _, n_blocks, _, te = ut4.shape

    if rows_per_mod is None:
        g2_idx = lambda i, e: (mod_row0 * 6 + 5, 0, 0)
    else:
        g2_idx = lambda i, e: ((mod_row0 + (i * tm) // rows_per_mod) * 6 + 5, 0, 0)
    tok = lambda cols: pl.BlockSpec((tm, cols), lambda i, e: (i, 0))
    const = lambda shape: pl.BlockSpec(shape, lambda i, e: (0,) * len(shape))
    return pl.pallas_call(
        functools.partial(_dense_kernel, alpha=alpha),
        grid=(n // tm, n_blocks),
        in_specs=[
            tok(d_model),
            pl.BlockSpec((tm, te), lambda i, e: (i, e)),
            pl.BlockSpec((1, 1, d_model, te), lambda i, e: (layer, e, 0, 0)),
            pl.BlockSpec((1, te, d_model), lambda i, e: (layer, e, 0)),
            tok(d_model),
            pl.BlockSpec((1, 1, d_model), g2_idx),
            const((1, d_model)), const((1, d_model)),
        ],
        out_specs=tok(d_model),
        out_shape=jax.ShapeDtypeStruct((n, d_model), F32),
        compiler_params=_cparams(("parallel", "arbitrary")),
        name="dense",
    )(h16, gates, ut4, v16, x2d, mods3, ln_g.reshape(1, d_model), ln_b.reshape(1, d_model))


def kernel(x, c, ctx, c_ctx, w_mod, b_mod, w_in, conv_w, conv_b, lb_raw, hg_norm_g, w_pa, w_pb, w_o,
           ln1_g, ln1_b, peer_wq, peer_keys, peer_u, peer_v, ln2_g, ln2_b):
    batch, s_len, d_model = x.shape
    c_len = ctx.shape[1]
    depth = w_mod.shape[0]
    alpha = (2.0 * depth) ** 0.25
    n_lat = batch * s_len
    n_ctx = batch * c_len

    p = jax.nn.softmax(lb_raw.astype(F32), axis=0)
    lower = jnp.cumsum(p, axis=0) - p[:1]

    cond = jnp.concatenate([c, c_ctx[None, :]], axis=0)
    cb = jnp.broadcast_to(cond.T[:, :, None], (d_model, batch + 1, LANES)).reshape(d_model, -1)
    mods = _mod_call(cb, w_mod, b_mod, batch + 1)

    w_in16 = _cast_call(w_in)
    v16 = _cast_call(peer_v)
    ut4 = _cast_t_call(peer_u, DENSE_TE)

    xl = x.reshape(n_lat, d_model)
    xc = ctx.reshape(n_ctx, d_model)
    for l in range(depth):
        last = l == depth - 1
        mods3 = mods[l].reshape(16 * 6, 1, d_model)
        wpa, wpb, wo = w_pa[l].astype(BF16), w_pb[l].astype(BF16), w_o[l].astype(BF16)
        wq = peer_wq[l].astype(BF16)
        keys = peer_keys[l].reshape(PEER_HEADS * 2, N_KEYS, -1).astype(BF16)

        p_lat = _inproj_call(xl, mods3, w_in16, l, s_len, 0, 1024)
        p_ctx = _inproj_call(xc, mods3, w_in16, l, None, batch, 1024)
        yb_lat, yb_ctx = _hgrn_call(p_lat, p_ctx, lower[l], hg_norm_g[l], batch, d_model)

        streams = [(xl, p_lat, yb_lat, s_len, 0, GRID_W)]
        if not last:
            streams.append((xc, p_ctx, yb_ctx, None, batch, c_len))
        outs = []
        for xs, ps, ybs, rpm, row0, row_len in streams:
            x1 = _merge_call(ps, ybs, xs, mods3, conv_w[l], conv_b[l], wpa, wpb, wo, ln1_g[l], ln1_b[l],
                             rpm, row0, row_len, 256, alpha)
            h16, ii, jj, gg = _route_call(x1, mods3, wq, keys, rpm, row0, 256)
            gates = _gates_call(ii, jj, gg, 256)
            outs.append(_dense_call(h16, gates, ut4, v16, l, x1, mods3, ln2_g[l], ln2_b[l],
                                    rpm, row0, DENSE_TM, alpha))
        xl = outs[0]
        if not last:
            xc = outs[1]
    return xl.reshape(batch, s_len, d_model)
```

```python
import functools
import math

import numpy as np
import jax
import jax.numpy as jnp
from jax import lax
from jax.experimental import pallas as pl
from jax.experimental.pallas import tpu as pltpu

F32 = jnp.float32
BF16 = jnp.bfloat16

LANES = 128
SUBLANES = 8
VMEM_LIMIT = 56 * 1024 * 1024

EPS = 1e-6
F_MIN = 1e-30
GRID_W = 64
HG_DK = 128
HG_DV = 128
CHUNK = 64
LEVELS = (32, 16, 8, 4, 2, 1, 0)
MATMUL_LEVELS = (4, 2)
INTRA_CHUNKS = 4
N_KEYS = 128
TOPK = 16
PEER_HEADS = 8
G_PITCH = 132
GATE_UNROLL = 32
DENSE_TE = 1024
DENSE_TM = 512
CAST_BLOCK_BYTES = 4 * 1024 * 1024


def _cparams(sem):
    return pltpu.CompilerParams(dimension_semantics=sem, vmem_limit_bytes=VMEM_LIMIT)


def _tile(n, pref):
    t = min(n, pref)
    while n % t:
        t //= 2
    return t


def _ln_rows(x):
    mu = jnp.mean(x, axis=-1, keepdims=True)
    xc = x - mu
    var = jnp.mean(xc * xc, axis=-1, keepdims=True)
    return xc * lax.rsqrt(var + EPS)


def _sigmoid(x):
    return 1.0 / (1.0 + jnp.exp(-x))


def _silu(x):
    return x * _sigmoid(x)


def _split_bf16(x):
    hi = x.astype(BF16)
    lo = (x - hi.astype(F32)).astype(BF16)
    return hi, lo


def _cast_kernel(x_ref, o_ref):
    o_ref[...] = x_ref[...].astype(o_ref.dtype)


def _cast_call(w):
    depth, r, c = w.shape
    rows = _tile(r, 1 << max(4, (CAST_BLOCK_BYTES // (4 * c)).bit_length() - 1))
    spec = pl.BlockSpec((1, rows, c), lambda l, i: (l, i, 0))
    return pl.pallas_call(
        _cast_kernel, grid=(depth, r // rows), in_specs=[spec], out_specs=spec,
        out_shape=jax.ShapeDtypeStruct(w.shape, BF16),
        compiler_params=_cparams(("parallel", "parallel")), name="cast",
    )(w)


def _cast_t_kernel(x_ref, o_ref):
    d = x_ref.shape[2]
    step = 256 if d % 256 == 0 else d
    for c in range(0, d, step):
        o_ref[0, 0, c:c + step, :] = x_ref[0, :, c:c + step].T.astype(o_ref.dtype)


def _cast_t_call(w, te):
    depth, e, d = w.shape
    return pl.pallas_call(
        _cast_t_kernel, grid=(depth, e // te),
        in_specs=[pl.BlockSpec((1, te, d), lambda l, i: (l, i, 0))],
        out_specs=pl.BlockSpec((1, 1, d, te), lambda l, i: (l, i, 0, 0)),
        out_shape=jax.ShapeDtypeStruct((depth, e // te, d, te), BF16),
        compiler_params=_cparams(("parallel", "parallel")), name="cast_t",
    )(w)


def _mod_kernel(cb_ref, w_ref, b_ref, o_ref, s_tab, *, n_rows):
    d_model = w_ref.shape[1]
    tn = w_ref.shape[2]

    @pl.when((pl.program_id(0) == 0) & (pl.program_id(1) == 0))
    def _():
        def fill(g, carry):
            r0 = pl.multiple_of(g * 64, 64)
            s_tab[pl.ds(r0, 64), :] = _silu(cb_ref[pl.ds(r0, 64), :])
            return carry

        lax.fori_loop(0, d_model // 64, fill, 0)

    def body(g, accs):
        r0 = pl.multiple_of(g * SUBLANES, SUBLANES)
        w8 = w_ref[0, pl.ds(r0, SUBLANES), :]
        out = []
        for r, acc in enumerate(accs):
            s8 = s_tab[pl.ds(r0, SUBLANES), r * LANES:(r + 1) * LANES]
            out.append(acc + jnp.concatenate([s8] * (tn // LANES), axis=1) * w8)
        return tuple(out)

    init = tuple(jnp.zeros((SUBLANES, tn), F32) for _ in range(n_rows))
    accs = lax.fori_loop(0, d_model // SUBLANES, body, init, unroll=2)
    bias = b_ref[0]
    o_ref[0] = jnp.zeros((o_ref.shape[1], tn), F32)
    for r, acc in enumerate(accs):
        o_ref[0, r:r + 1, :] = jnp.sum(acc, axis=0, keepdims=True) + bias


def _mod_call(cb, w_mod, b_mod, n_rows):
    depth, d_model, width = w_mod.shape
    tn = 512
    return pl.pallas_call(
        functools.partial(_mod_kernel, n_rows=n_rows),
        grid=(depth, width // tn),
        in_specs=[
            pl.BlockSpec((d_model, n_rows * LANES), lambda l, j: (0, 0), pipeline_mode=pl.Buffered(1)),
            pl.BlockSpec((1, d_model, tn), lambda l, j: (l, 0, j)),
            pl.BlockSpec((1, 1, tn), lambda l, j: (l, 0, j)),
        ],
        out_specs=pl.BlockSpec((1, 16, tn), lambda l, j: (l, 0, j)),
        out_shape=jax.ShapeDtypeStruct((depth, 16, width), F32),
        scratch_shapes=[pltpu.VMEM((d_model, n_rows * LANES), F32)],
        compiler_params=_cparams(("arbitrary", "arbitrary")),
        name="mod",
    )(cb, w_mod, b_mod.reshape(depth, 1, width))


def _inproj_kernel(x_ref, sh_ref, sc_ref, w_ref, o_ref, h_ref):
    @pl.when(pl.program_id(1) == 0)
    def _():
        y = _ln_rows(x_ref[...])
        h_ref[...] = (y * (1.0 + sc_ref[0]) + sh_ref[0]).astype(BF16)

    o_ref[...] = jnp.dot(h_ref[...], w_ref[0], preferred_element_type=F32)


def _inproj_call(x2d, mods3, w_bf16, layer, rows_per_mod, mod_row0, tm):
    n, d_model = x2d.shape
    tm = _tile(rows_per_mod or n, tm)
    width = w_bf16.shape[2]
    tn = _tile(width, 1024)

    def mod_idx(g):
        if rows_per_mod is None:
            return lambda i, j: (mod_row0 * 6 + g, 0, 0)
        return lambda i, j: ((mod_row0 + (i * tm) // rows_per_mod) * 6 + g, 0, 0)

    return pl.pallas_call(
        _inproj_kernel,
        grid=(n // tm, width // tn),
        in_specs=[
            pl.BlockSpec((tm, d_model), lambda i, j: (i, 0)),
            pl.BlockSpec((1, 1, d_model), mod_idx(0)),
            pl.BlockSpec((1, 1, d_model), mod_idx(1)),
            pl.BlockSpec((1, d_model, tn), lambda i, j: (layer, 0, j)),
        ],
        out_specs=pl.BlockSpec((tm, tn), lambda i, j: (i, j)),
        out_shape=jax.ShapeDtypeStruct((n, width), F32),
        scratch_shapes=[pltpu.VMEM((tm, d_model), BF16)],
        compiler_params=_cparams(("parallel", "arbitrary")),
        name="inproj",
    )(x2d, mods3, mods3, w_bf16)


def _decay_matrices(reverse):
    L = CHUNK
    rank = np.arange(L)[::-1] if reverse else np.arange(L)
    rt = rank[:, None]
    rr = rank[None, :]
    blocks = [(rr <= rt)]
    for h in MATMUL_LEVELS:
        upper = (rt & h) != 0
        blocks.append(upper & (rr >= (rt & ~(h - 1))) & (rr <= rt))
        blocks.append((~upper) & (rr > rt) & (rr <= (rt | (h - 1))))
    return np.concatenate(blocks, axis=0).astype(np.float32)


def _half_block_exponents(bcum, h, reverse):
    eq, ek = [], []
    zero = jnp.zeros((h, LANES), F32)
    for b in range(CHUNK // h):
        rows = bcum[b * h:(b + 1) * h]
        first_of_pair = b % 2 == 0
        if not reverse:
            if first_of_pair:
                eq.append(zero)
                ek.append(bcum[(b + 1) * h - 1:(b + 1) * h] - rows)
            else:
                eq.append(rows - bcum[b * h - 1:b * h])
                ek.append(zero)
        else:
            if first_of_pair:
                eq.append(rows - bcum[(b + 1) * h:(b + 1) * h + 1])
                ek.append(zero)
            else:
                eq.append(zero)
                ek.append(bcum[b * h:b * h + 1] - rows)
    return jnp.concatenate(eq, axis=0), jnp.concatenate(ek, axis=0)


def _level_masks(reverse):
    L = CHUNK
    rank = np.arange(L)[::-1] if reverse else np.arange(L)
    rt = rank[:, None]
    rs = rank[None, :]
    out = []
    for h in LEVELS:
        if h == 0:
            out.append(rt == rs)
        else:
            out.append(((rt ^ rs) // h == 1) & (rt > rs))
    return np.stack(out).astype(np.float32)


def _hgrn_intra(chains):
    L = CHUNK
    gs, ks, es, bcums, blasts = [], [], [], [], []
    for qs, z, v16, lb, dmat, masks, reverse in chains:
        sig = _sigmoid(z)
        f = lb + (1.0 - lb) * sig
        g = jnp.log(jnp.maximum(f, F_MIN))
        g_hi, g_lo = _split_bf16(g)
        e2 = jnp.dot(dmat, jnp.concatenate([g_hi, g_lo], axis=1), preferred_element_type=F32)
        e = e2[:, :LANES] + e2[:, LANES:]
        gs.append(g)
        ks.append((1.0 - lb) * (1.0 - sig))
        es.append(e)
        bcums.append(e[0:L])
        blasts.append(e[0:1] if reverse else e[L - 1:L])

    scores = [jnp.zeros((L, L), F32) for _ in chains]
    for i, h in enumerate(LEVELS):
        for c, (qs, z, v16, lb, dmat, masks, reverse) in enumerate(chains):
            k = ks[c]
            if h <= 1:
                if h == 0:
                    prod = qs * k
                else:
                    prod = qs * jnp.exp(gs[c]) * pltpu.roll(k, L - 1 if reverse else 1, axis=0)
                scores[c] = scores[c] + masks[i] * jnp.sum(prod, axis=-1, keepdims=True)
                continue
            if h in MATMUL_LEVELS:
                m = 1 + 2 * MATMUL_LEVELS.index(h)
                qt = qs * jnp.exp(es[c][m * L:(m + 1) * L])
                kt = k * jnp.exp(es[c][(m + 1) * L:(m + 2) * L])
            else:
                eq, ek = _half_block_exponents(bcums[c], h, reverse)
                qt = qs * jnp.exp(eq)
                kt = k * jnp.exp(ek)
            s_h = lax.dot_general(qt.astype(BF16), kt.astype(BF16), (((1,), (1,)), ((), ())),
                                  preferred_element_type=F32)
            scores[c] = scores[c] + masks[i] * s_h

    outs = []
    for c, (qs, z, v16, lb, dmat, masks, reverse) in enumerate(chains):
        o = jnp.dot(scores[c].astype(BF16), v16, preferred_element_type=F32)
        qhat = (qs * jnp.exp(bcums[c])).astype(BF16)
        kvt = lax.dot_general(v16, (ks[c] * jnp.exp(blasts[c] - bcums[c])).astype(BF16),
                              (((0,), (0,)), ((), ())), preferred_element_type=F32)
        outs.append((o, qhat, kvt, blasts[c]))
    return outs


def _hgrn_kernel(ql, zfl, zbl, vl, ogl, qc, zfc, zbc, vc, ogc, lb_ref, ng_ref, dm_ref, mk_ref,
                 yl_ref, yc_ref, ol_s, oc_s, qh_s, kv_s, bl_s, st_s):
    lbf = lb_ref[0:1, :]
    lbb = lb_ref[1:2, :]

    def run(q_ref, zf_ref, zb_ref, v_ref, o_s, states):
        n = q_ref.shape[0] // CHUNK
        per_it = math.gcd(n, INTRA_CHUNKS)

        def intra(it, carry):
            chains = []
            for u in range(per_it):
                c = it * per_it + u
                r = pl.multiple_of(c * CHUNK, CHUNK)
                qs = _silu(q_ref[pl.ds(r, CHUNK), :]) * (HG_DK ** -0.5)
                v16 = v_ref[pl.ds(r, CHUNK), :].astype(BF16)
                chains.append((qs, zf_ref[pl.ds(r, CHUNK), :], v16, lbf, dm_ref[0], mk_ref[0], False))
                chains.append((qs, zb_ref[pl.ds(r, CHUNK), :], v16, lbb, dm_ref[1], mk_ref[1], True))
            res = _hgrn_intra(chains)
            for u in range(per_it):
                c = it * per_it + u
                r = pl.multiple_of(c * CHUNK, CHUNK)
                (of, qf, kvf, blf), (ob, qb, kvb, blb) = res[2 * u], res[2 * u + 1]
                o_s[pl.ds(r, CHUNK), :] = of + ob
                qh_s[pl.ds(r, CHUNK), 0:HG_DK] = qf
                qh_s[pl.ds(r, CHUNK), HG_DK:2 * HG_DK] = qb
                kv_s[0, c] = kvf
                kv_s[1, c] = kvb
                bl_s[0, c] = jnp.broadcast_to(blf, (SUBLANES, HG_DK))
                bl_s[1, c] = jnp.broadcast_to(blb, (SUBLANES, HG_DK))
            return carry

        lax.fori_loop(0, n // per_it, intra, 0)

        def recur(c, carry):
            sf, sb = carry
            cb = n - 1 - c
            st_s[c, :, 0:HG_DK] = sf.astype(BF16)
            st_s[cb, :, HG_DK:2 * HG_DK] = sb.astype(BF16)
            sf = sf * jnp.exp(bl_s[0, c][0:1, :]) + kv_s[0, c]
            sb = sb * jnp.exp(bl_s[1, cb][0:1, :]) + kv_s[1, cb]
            return sf, sb

        states = lax.fori_loop(0, n, recur, states)

        def inter(c, carry):
            r = pl.multiple_of(c * CHUNK, CHUNK)
            o_s[pl.ds(r, CHUNK), :] += lax.dot_general(
                qh_s[pl.ds(r, CHUNK), :], st_s[c], (((1,), (1,)), ((), ())), preferred_element_type=F32)
            return carry

        lax.fori_loop(0, n, inter, 0, unroll=4)
        return states

    zero = jnp.zeros((HG_DV, HG_DK), F32)
    states = run(qc, zfc, zbc, vc, oc_s, (zero, zero))
    run(ql, zfl, zbl, vl, ol_s, states)

    def readout(o_s, og_ref, y_ref):
        o = o_s[...]
        o = o * lax.rsqrt(jnp.mean(o * o, axis=-1, keepdims=True) + EPS) * ng_ref[...]
        y_ref[...] = (o * _silu(og_ref[...])).astype(y_ref.dtype)

    readout(ol_s, ogl, yl_ref)
    readout(oc_s, ogc, yc_ref)


def _hgrn_call(p_lat, p_ctx, lb, norm_g, batch, d_model):
    n_lat = p_lat.shape[0]
    n_ctx = p_ctx.shape[0]
    s_len = n_lat // batch
    c_len = n_ctx // batch
    d_half = d_model // 2
    heads = d_half // HG_DK
    cb = d_half // LANES
    q0, zf0, zb0, v0, og0 = 3 * cb, 4 * cb, 5 * cb, 6 * cb, 7 * cb

    def spec(rows, c0):
        return pl.BlockSpec((rows, LANES), lambda b, h, c0=c0: (b, c0 + h))

    dmat = jnp.asarray(np.stack([_decay_matrices(False), _decay_matrices(True)]), BF16)
    masks = jnp.asarray(np.stack([_level_masks(False), _level_masks(True)]), F32)
    full = lambda shape: pl.BlockSpec(shape, lambda b, h: (0,) * len(shape))

    return pl.pallas_call(
        _hgrn_kernel,
        grid=(batch, heads),
        in_specs=[spec(s_len, c0) for c0 in (q0, zf0, zb0, v0, og0)]
        + [spec(c_len, c0) for c0 in (q0, zf0, zb0, v0, og0)]
        + [pl.BlockSpec((2, LANES), lambda b, h: (0, h)),
           full((1, HG_DV)), full(dmat.shape), full(masks.shape)],
        out_specs=[pl.BlockSpec((s_len, LANES), lambda b, h: (b, h)),
                   pl.BlockSpec((c_len, LANES), lambda b, h: (b, h))],
        out_shape=[jax.ShapeDtypeStruct((n_lat, d_half), BF16),
                   jax.ShapeDtypeStruct((n_ctx, d_half), BF16)],
        scratch_shapes=[pltpu.VMEM((s_len, HG_DV), F32), pltpu.VMEM((c_len, HG_DV), F32),
                        pltpu.VMEM((s_len, 2 * HG_DK), BF16),
                        pltpu.VMEM((2, s_len // CHUNK, HG_DV, HG_DK), F32),
                        pltpu.VMEM((2, s_len // CHUNK, SUBLANES, HG_DK), F32),
                        pltpu.VMEM((s_len // CHUNK, HG_DV, 2 * HG_DK), BF16)],
        compiler_params=_cparams(("parallel", "parallel")),
        name="hgrn",
    )(*([p_lat] * 5), *([p_ctx] * 5), lb, norm_g.reshape(1, HG_DV), dmat, masks)


def _merge_kernel(cb_ref, cc_ref, cv_ref, ga_ref, gb_ref, yb_ref, x_ref, g1_ref, cw_ref, cbias_ref,
                  wpa_ref, wpb_ref, wo_ref, lng_ref, lnb_ref, o_ref, *, row_len, alpha):
    tm = x_ref.shape[0]
    u = cc_ref[...] * cv_ref[...]
    pos = lax.broadcasted_iota(jnp.int32, (tm, 1), 0) % row_len
    prev = jnp.where(pos == 0, 0.0, pltpu.roll(u, 1, axis=0))
    nxt = jnp.where(pos == row_len - 1, 0.0, pltpu.roll(u, tm - 1, axis=0))
    conv = prev * cw_ref[0:1, :] + u * cw_ref[1:2, :] + nxt * cw_ref[2:3, :] + cbias_ref[...]
    ya = (cb_ref[...] * conv).astype(BF16)
    m = (_sigmoid(ga_ref[...]) * jnp.dot(ya, wpa_ref[...], preferred_element_type=F32)
         + _sigmoid(gb_ref[...]) * jnp.dot(yb_ref[...], wpb_ref[...], preferred_element_type=F32))
    y = jnp.dot(m.astype(BF16), wo_ref[...], preferred_element_type=F32)
    r = alpha * x_ref[...] + g1_ref[0] * y
    o_ref[...] = _ln_rows(r) * lng_ref[...] + lnb_ref[...]


def _merge_call(p, yb, x2d, mods3, conv_w, conv_b, wpa, wpb, wo, ln_g, ln_b, rows_per_mod, mod_row0,
                row_len, tm, alpha):
    n, d_model = x2d.shape
    tm = _tile(rows_per_mod or n, tm)
    d_half = d_model // 2

    if rows_per_mod is None:
        g1_idx = lambda i: (mod_row0 * 6 + 2, 0, 0)
    else:
        g1_idx = lambda i: ((mod_row0 + (i * tm) // rows_per_mod) * 6 + 2, 0, 0)
    const = lambda shape: pl.BlockSpec(shape, lambda i: (0,) * len(shape), pipeline_mode=pl.Buffered(1))
    return pl.pallas_call(
        functools.partial(_merge_kernel, row_len=row_len, alpha=alpha),
        grid=(n // tm,),
        in_specs=[
            pl.BlockSpec((tm, d_half), lambda i: (i, 0)),
            pl.BlockSpec((tm, d_half), lambda i: (i, 1)),
            pl.BlockSpec((tm, d_half), lambda i: (i, 2)),
            pl.BlockSpec((tm, d_model), lambda i: (i, 4)),
            pl.BlockSpec((tm, d_model), lambda i: (i, 5)),
            pl.BlockSpec((tm, d_half), lambda i: (i, 0)),
            pl.BlockSpec((tm, d_model), lambda i: (i, 0)),
            pl.BlockSpec((1, 1, d_model), g1_idx),
            const((3, d_half)), const((1, d_half)),
            const((d_half, d_model)), const((d_half, d_model)), const((d_model, d_model)),
            const((1, d_model)), const((1, d_model)),
        ],
        out_specs=pl.BlockSpec((tm, d_model), lambda i: (i, 0)),
        out_shape=jax.ShapeDtypeStruct((n, d_model), F32),
        compiler_params=_cparams(("parallel",)),
        name="merge",
    )(p, p, p, p, p, yb, x2d, mods3, conv_w, conv_b.reshape(1, d_half), wpa, wpb, wo,
      ln_g.reshape(1, d_model), ln_b.reshape(1, d_model))


def _topk_rows(s, k, payload=None):
    rows = s.shape[0]
    riota = lax.broadcasted_iota(jnp.int32, s.shape, 0).astype(F32)
    vals, picks = [], []
    for _ in range(k):
        m = jnp.max(s, axis=0, keepdims=True)
        am = jnp.min(jnp.where(s == m, riota, float(rows)), axis=0, keepdims=True)
        sel = riota == am
        vals.append(m)
        if payload is None:
            picks.append(am)
        else:
            picks.append(jnp.max(jnp.where(sel, payload, -1.0), axis=0, keepdims=True))
        s = jnp.where(sel, -jnp.inf, s)
    return jnp.concatenate(vals, axis=0), jnp.concatenate(picks, axis=0)


def _pair_candidates():
    return [(a, min(TOPK, TOPK // (a + 1))) for a in range(TOPK)]


def _route_kernel(x_ref, sh_ref, sc_ref, wq_ref, keys_ref, h_ref, ii_ref, jj_ref, gg_ref):
    hmod = _ln_rows(x_ref[...]) * (1.0 + sc_ref[0]) + sh_ref[0]
    h16 = hmod.astype(BF16)
    h_ref[...] = h16
    qp = jnp.dot(h16, wq_ref[...], preferred_element_type=F32).astype(BF16)
    i_rows, j_rows, g_rows = [], [], []
    for head in range(PEER_HEADS):
        sv, si = [], []
        for half in range(2):
            c = (head * 2 + half) * LANES
            s = lax.dot_general(keys_ref[head * 2 + half], qp[:, c:c + LANES], (((1,), (1,)), ((), ())),
                                preferred_element_type=F32)
            v_, i_ = _topk_rows(s, TOPK)
            sv.append(v_)
            si.append(i_)
        pairs = _pair_candidates()
        n_pad = -sum(nb for _, nb in pairs) % SUBLANES
        tm = s.shape[1]
        cand = jnp.concatenate([sv[0][a:a + 1, :] + sv[1][0:nb, :] for a, nb in pairs]
                               + [jnp.full((n_pad, tm), -jnp.inf, F32)], axis=0)
        cidx = jnp.concatenate([si[0][a:a + 1, :] * float(N_KEYS) + si[1][0:nb, :] for a, nb in pairs]
                               + [jnp.zeros((n_pad, tm), F32)], axis=0)
        tv, te = _topk_rows(cand, TOPK, payload=cidx)
        te = te.astype(jnp.int32)
        ex = jnp.exp(tv - tv[0:1, :])
        g_rows.append(ex / jnp.sum(ex, axis=0, keepdims=True))
        i_rows.append(lax.shift_right_logical(te, 7))
        j_rows.append(te & (N_KEYS - 1))
    ii_ref[...] = jnp.concatenate(i_rows, axis=0).T
    jj_ref[...] = jnp.concatenate(j_rows, axis=0).T
    gg_ref[...] = jnp.concatenate(g_rows, axis=0).T


def _route_call(x2d, mods3, wq, keys, rows_per_mod, mod_row0, tm):
    n, d_model = x2d.shape
    tm = _tile(rows_per_mod or n, tm)
    slots = PEER_HEADS * TOPK

    def mod_idx(g):
        if rows_per_mod is None:
            return lambda i: (mod_row0 * 6 + g, 0, 0)
        return lambda i: ((mod_row0 + (i * tm) // rows_per_mod) * 6 + g, 0, 0)

    const = lambda shape: pl.BlockSpec(shape, lambda i: (0,) * len(shape), pipeline_mode=pl.Buffered(1))
    return pl.pallas_call(
        _route_kernel,
        grid=(n // tm,),
        in_specs=[
            pl.BlockSpec((tm, d_model), lambda i: (i, 0)),
            pl.BlockSpec((1, 1, d_model), mod_idx(3)),
            pl.BlockSpec((1, 1, d_model), mod_idx(4)),
            const(wq.shape), const(keys.shape),
        ],
        out_specs=[pl.BlockSpec((tm, d_model), lambda i: (i, 0))]
        + [pl.BlockSpec((tm, slots), lambda i: (i, 0))] * 3,
        out_shape=[jax.ShapeDtypeStruct((n, d_model), BF16),
                   jax.ShapeDtypeStruct((n, slots), jnp.int32),
                   jax.ShapeDtypeStruct((n, slots), jnp.int32),
                   jax.ShapeDtypeStruct((n, slots), F32)],
        compiler_params=_cparams(("parallel",)),
        name="route",
    )(x2d, mods3, mods3, wq, keys)


def _gelu_exact(x):
    return 0.5 * x * (1.0 + lax.erf(x * (2.0 ** -0.5)))


def _gates_kernel(ii_ref, jj_ref, gg_ref, o_ref, gate_s):
    tm = ii_ref.shape[0]
    sub = lax.broadcasted_iota(jnp.int32, (N_KEYS, N_KEYS), 0)

    def tokens(it, carry):
        base = pl.multiple_of(it * (GATE_UNROLL * G_PITCH), SUBLANES)
        for u in range(GATE_UNROLL):
            n = it * GATE_UNROLL + u
            ii = ii_ref[pl.ds(n, 1), :]
            jj = jj_ref[pl.ds(n, 1), :]
            gg = gg_ref[pl.ds(n, 1), :]
            a_t = jnp.where(ii == sub, gg, 0.0).astype(BF16)
            b = jnp.where(jj == sub, 1.0, 0.0).T.astype(BF16)
            gate = jnp.dot(a_t, b, preferred_element_type=F32)
            gate_s[pl.ds(base + u * G_PITCH, N_KEYS), :] = gate
        return carry

    lax.fori_loop(0, tm // GATE_UNROLL, tokens, 0)
    for i in range(N_KEYS):
        o_ref[:, i * N_KEYS:(i + 1) * N_KEYS] = gate_s[pl.ds(i, tm, stride=G_PITCH), :].astype(o_ref.dtype)


def _gates_call(ii, jj, gg, tm):
    n, slots = ii.shape
    tm = _tile(n, tm)
    tok = pl.BlockSpec((tm, slots), lambda i: (i, 0))
    return pl.pallas_call(
        _gates_kernel,
        grid=(n // tm,),
        in_specs=[tok, tok, tok],
        out_specs=pl.BlockSpec((tm, N_KEYS * N_KEYS), lambda i: (i, 0)),
        out_shape=jax.ShapeDtypeStruct((n, N_KEYS * N_KEYS), BF16),
        scratch_shapes=[pltpu.VMEM((tm * G_PITCH, LANES), F32)],
        compiler_params=_cparams(("parallel",)),
        name="gates",
    )(ii, jj, gg)


def _dense_kernel(h_ref, gate_ref, ut_ref, v_ref, x_ref, g2_ref, lng_ref, lnb_ref, o_ref, *, alpha):
    eb = pl.program_id(1)

    @pl.when(eb == 0)
    def _():
        o_ref[...] = jnp.zeros(o_ref.shape, F32)

    act = _gelu_exact(jnp.dot(h_ref[...], ut_ref[0, 0], preferred_element_type=F32))
    hg = (gate_ref[...].astype(F32) * act).astype(BF16)
    o_ref[...] += jnp.dot(hg, v_ref[0], preferred_element_type=F32)

    @pl.when(eb == pl.num_programs(1) - 1)
    def _():
        r = alpha * x_ref[...] + g2_ref[0] * o_ref[...]
        o_ref[...] = _ln_rows(r) * lng_ref[...] + lnb_ref[...]


def _dense_call(h16, gates, ut4, v16, layer, x2d, mods3, ln_g, ln_b, rows_per_mod, mod_row0, tm, alpha):
    n, d_model = x2d.shape
    tm = _tile(rows_per_mod or n, tm)
    _, n_blocks, _, te = ut4.shape

    if rows_per_mod is None:
        g2_idx = lambda i, e: (mod_row0 * 6 + 5, 0, 0)
    else:
        g2_idx = lambda i, e: ((mod_row0 + (i * tm) // rows_per_mod) * 6 + 5, 0, 0)
    tok = lambda cols: pl.BlockSpec((tm, cols), lambda i, e: (i, 0))
    const = lambda shape: pl.BlockSpec(shape, lambda i, e: (0,) * len(shape))
    return pl.pallas_call(
        functools.partial(_dense_kernel, alpha=alpha),
        grid=(n // tm, n_blocks),
        in_specs=[
            tok(d_model),
            pl.BlockSpec((tm, te), lambda i, e: (i, e)),
            pl.BlockSpec((1, 1, d_model, te), lambda i, e: (layer, e, 0, 0)),
            pl.BlockSpec((1, te, d_model), lambda i, e: (layer, e, 0)),
            tok(d_model),
            pl.BlockSpec((1, 1, d_model), g2_idx),
            const((1, d_model)), const((1, d_model)),
        ],
        out_specs=tok(d_model),
        out_shape=jax.ShapeDtypeStruct((n, d_model), F32),
        compiler_params=_cparams(("parallel", "arbitrary")),
        name="dense",
    )(h16, gates, ut4, v16, x2d, mods3, ln_g.reshape(1, d_model), ln_b.reshape(1, d_model))


def kernel(x, c, ctx, c_ctx, w_mod, b_mod, w_in, conv_w, conv_b, lb_raw, hg_norm_g, w_pa, w_pb, w_o,
           ln1_g, ln1_b, peer_wq, peer_keys, peer_u, peer_v, ln2_g, ln2_b):
    batch, s_len, d_model = x.shape
    c_len = ctx.shape[1]
    depth = w_mod.shape[0]
    alpha = (2.0 * depth) ** 0.25
    n_lat = batch * s_len
    n_ctx = batch * c_len

    p = jax.nn.softmax(lb_raw.astype(F32), axis=0)
    lower = jnp.cumsum(p, axis=0) - p[:1]

    cond = jnp.concatenate([c, c_ctx[None, :]], axis=0)
    cb = jnp.broadcast_to(cond.T[:, :, None], (d_model, batch + 1, LANES)).reshape(d_model, -1)
    mods = _mod_call(cb, w_mod, b_mod, batch + 1)

    w_in16 = _cast_call(w_in)
    v16 = _cast_call(peer_v)
    ut4 = _cast_t_call(peer_u, DENSE_TE)

    xl = x.reshape(n_lat, d_model)
    xc = ctx.reshape(n_ctx, d_model)
    for l in range(depth):
        last = l == depth - 1
        mods3 = mods[l].reshape(16 * 6, 1, d_model)
        wpa, wpb, wo = w_pa[l].astype(BF16), w_pb[l].astype(BF16), w_o[l].astype(BF16)
        wq = peer_wq[l].astype(BF16)
        keys = peer_keys[l].reshape(PEER_HEADS * 2, N_KEYS, -1).astype(BF16)

        p_lat = _inproj_call(xl, mods3, w_in16, l, s_len, 0, 1024)
        p_ctx = _inproj_call(xc, mods3, w_in16, l, None, batch, 1024)
        yb_lat, yb_ctx = _hgrn_call(p_lat, p_ctx, lower[l], hg_norm_g[l], batch, d_model)

        streams = [(xl, p_lat, yb_lat, s_len, 0, GRID_W)]
        if not last:
            streams.append((xc, p_ctx, yb_ctx, None, batch, c_len))
        outs = []
        for xs, ps, ybs, rpm, row0, row_len in streams:
            x1 = _merge_call(ps, ybs, xs, mods3, conv_w[l], conv_b[l], wpa, wpb, wo, ln1_g[l], ln1_b[l],
                             rpm, row0, row_len, 256, alpha)
            h16, ii, jj, gg = _route_call(x1, mods3, wq, keys, rpm, row0, 256)
            gates = _gates_call(ii, jj, gg, 256)
            outs.append(_dense_call(h16, gates, ut4, v16, l, x1, mods3, ln2_g[l], ln2_b[l],
                                    rpm, row0, DENSE_TM, alpha))
        xl = outs[0]
        if not last:
            xc = outs[1]
    return xl.reshape(batch, s_len, d_model)
```

```python
import functools
import math

import numpy as np
import jax
import jax.numpy as jnp
from jax import lax
from jax.experimental import pallas as pl
from jax.experimental.pallas import tpu as pltpu

F32 = jnp.float32
BF16 = jnp.bfloat16

LANES = 128
SUBLANES = 8
VMEM_LIMIT = 56 * 1024 * 1024

EPS = 1e-6
F_MIN = 1e-30
GRID_W = 64
HG_DK = 128
HG_DV = 128
CHUNK = 64
LEVELS = (32, 16, 8, 4, 2, 1, 0)
MATMUL_LEVELS = (4, 2)
INTRA_CHUNKS = 4
N_KEYS = 128
TOPK = 16
PEER_HEADS = 8
G_PITCH = 132
GATE_UNROLL = 64
DENSE_TE = 1024
DENSE_TM = 512
CAST_BLOCK_BYTES = 4 * 1024 * 1024


def _cparams(sem):
    return pltpu.CompilerParams(dimension_semantics=sem, vmem_limit_bytes=VMEM_LIMIT)


def _tile(n, pref):
    t = min(n, pref)
    while n % t:
        t //= 2
    return t


def _ln_rows(x):
    mu = jnp.mean(x, axis=-1, keepdims=True)
    xc = x - mu
    var = jnp.mean(xc * xc, axis=-1, keepdims=True)
    return xc * lax.rsqrt(var + EPS)


def _sigmoid(x):
    return 1.0 / (1.0 + jnp.exp(-x))


def _silu(x):
    return x * _sigmoid(x)


def _split_bf16(x):
    hi = x.astype(BF16)
    lo = (x - hi.astype(F32)).astype(BF16)
    return hi, lo


def _cast_kernel(x_ref, o_ref):
    o_ref[...] = x_ref[...].astype(o_ref.dtype)


def _cast_call(w):
    depth, r, c = w.shape
    rows = _tile(r, 1 << max(4, (CAST_BLOCK_BYTES // (4 * c)).bit_length() - 1))
    spec = pl.BlockSpec((1, rows, c), lambda l, i: (l, i, 0))
    return pl.pallas_call(
        _cast_kernel, grid=(depth, r // rows), in_specs=[spec], out_specs=spec,
        out_shape=jax.ShapeDtypeStruct(w.shape, BF16),
        compiler_params=_cparams(("parallel", "parallel")), name="cast",
    )(w)


def _cast_t_kernel(x_ref, o_ref):
    d = x_ref.shape[2]
    step = 256 if d % 256 == 0 else d
    for c in range(0, d, step):
        o_ref[0, 0, c:c + step, :] = x_ref[0, :, c:c + step].T.astype(o_ref.dtype)


def _cast_t_call(w, te):
    depth, e, d = w.shape
    return pl.pallas_call(
        _cast_t_kernel, grid=(depth, e // te),
        in_specs=[pl.BlockSpec((1, te, d), lambda l, i: (l, i, 0))],
        out_specs=pl.BlockSpec((1, 1, d, te), lambda l, i: (l, i, 0, 0)),
        out_shape=jax.ShapeDtypeStruct((depth, e // te, d, te), BF16),
        compiler_params=_cparams(("parallel", "parallel")), name="cast_t",
    )(w)


def _mod_kernel(cb_ref, w_ref, b_ref, o_ref, s_tab, *, n_rows):
    d_model = w_ref.shape[1]
    tn = w_ref.shape[2]

    @pl.when((pl.program_id(0) == 0) & (pl.program_id(1) == 0))
    def _():
        def fill(g, carry):
            r0 = pl.multiple_of(g * 64, 64)
            s_tab[pl.ds(r0, 64), :] = _silu(cb_ref[pl.ds(r0, 64), :])
            return carry

        lax.fori_loop(0, d_model // 64, fill, 0)

    def body(g, accs):
        r0 = pl.multiple_of(g * SUBLANES, SUBLANES)
        w8 = w_ref[0, pl.ds(r0, SUBLANES), :]
        out = []
        for r, acc in enumerate(accs):
            s8 = s_tab[pl.ds(r0, SUBLANES), r * LANES:(r + 1) * LANES]
            out.append(acc + jnp.concatenate([s8] * (tn // LANES), axis=1) * w8)
        return tuple(out)

    init = tuple(jnp.zeros((SUBLANES, tn), F32) for _ in range(n_rows))
    accs = lax.fori_loop(0, d_model // SUBLANES, body, init, unroll=2)
    bias = b_ref[0]
    o_ref[0] = jnp.zeros((o_ref.shape[1], tn), F32)
    for r, acc in enumerate(accs):
        o_ref[0, r:r + 1, :] = jnp.sum(acc, axis=0, keepdims=True) + bias


def _mod_call(cb, w_mod, b_mod, n_rows):
    depth, d_model, width = w_mod.shape
    tn = 512
    return pl.pallas_call(
        functools.partial(_mod_kernel, n_rows=n_rows),
        grid=(depth, width // tn),
        in_specs=[
            pl.BlockSpec((d_model, n_rows * LANES), lambda l, j: (0, 0), pipeline_mode=pl.Buffered(1)),
            pl.BlockSpec((1, d_model, tn), lambda l, j: (l, 0, j)),
            pl.BlockSpec((1, 1, tn), lambda l, j: (l, 0, j)),
        ],
        out_specs=pl.BlockSpec((1, 16, tn), lambda l, j: (l, 0, j)),
        out_shape=jax.ShapeDtypeStruct((depth, 16, width), F32),
        scratch_shapes=[pltpu.VMEM((d_model, n_rows * LANES), F32)],
        compiler_params=_cparams(("arbitrary", "arbitrary")),
        name="mod",
    )(cb, w_mod, b_mod.reshape(depth, 1, width))


def _inproj_kernel(x_ref, sh_ref, sc_ref, w_ref, o_ref, h_ref):
    @pl.when(pl.program_id(1) == 0)
    def _():
        y = _ln_rows(x_ref[...])
        h_ref[...] = (y * (1.0 + sc_ref[0]) + sh_ref[0]).astype(BF16)

    o_ref[...] = jnp.dot(h_ref[...], w_ref[0], preferred_element_type=F32)


def _inproj_call(x2d, mods3, w_bf16, layer, rows_per_mod, mod_row0, tm):
    n, d_model = x2d.shape
    tm = _tile(rows_per_mod or n, tm)
    width = w_bf16.shape[2]
    tn = _tile(width, 1024)

    def mod_idx(g):
        if rows_per_mod is None:
            return lambda i, j: (mod_row0 * 6 + g, 0, 0)
        return lambda i, j: ((mod_row0 + (i * tm) // rows_per_mod) * 6 + g, 0, 0)

    return pl.pallas_call(
        _inproj_kernel,
        grid=(n // tm, width // tn),
        in_specs=[
            pl.BlockSpec((tm, d_model), lambda i, j: (i, 0)),
            pl.BlockSpec((1, 1, d_model), mod_idx(0)),
            pl.BlockSpec((1, 1, d_model), mod_idx(1)),
            pl.BlockSpec((1, d_model, tn), lambda i, j: (layer, 0, j)),
        ],
        out_specs=pl.BlockSpec((tm, tn), lambda i, j: (i, j)),
        out_shape=jax.ShapeDtypeStruct((n, width), F32),
        scratch_shapes=[pltpu.VMEM((tm, d_model), BF16)],
        compiler_params=_cparams(("parallel", "arbitrary")),
        name="inproj",
    )(x2d, mods3, mods3, w_bf16)


def _decay_matrices(reverse):
    L = CHUNK
    rank = np.arange(L)[::-1] if reverse else np.arange(L)
    rt = rank[:, None]
    rr = rank[None, :]
    blocks = [(rr <= rt)]
    for h in MATMUL_LEVELS:
        upper = (rt & h) != 0
        blocks.append(upper & (rr >= (rt & ~(h - 1))) & (rr <= rt))
        blocks.append((~upper) & (rr > rt) & (rr <= (rt | (h - 1))))
    return np.concatenate(blocks, axis=0).astype(np.float32)


def _half_block_exponents(bcum, h, reverse):
    eq, ek = [], []
    zero = jnp.zeros((h, LANES), F32)
    for b in range(CHUNK // h):
        rows = bcum[b * h:(b + 1) * h]
        first_of_pair = b % 2 == 0
        if not reverse:
            if first_of_pair:
                eq.append(zero)
                ek.append(bcum[(b + 1) * h - 1:(b + 1) * h] - rows)
            else:
                eq.append(rows - bcum[b * h - 1:b * h])
                ek.append(zero)
        else:
            if first_of_pair:
                eq.append(rows - bcum[(b + 1) * h:(b + 1) * h + 1])
                ek.append(zero)
            else:
                eq.append(zero)
                ek.append(bcum[b * h:b * h + 1] - rows)
    return jnp.concatenate(eq, axis=0), jnp.concatenate(ek, axis=0)


def _level_masks(reverse):
    L = CHUNK
    rank = np.arange(L)[::-1] if reverse else np.arange(L)
    rt = rank[:, None]
    rs = rank[None, :]
    out = []
    for h in LEVELS:
        if h == 0:
            out.append(rt == rs)
        else:
            out.append(((rt ^ rs) // h == 1) & (rt > rs))
    return np.stack(out).astype(np.float32)


def _hgrn_intra(chains):
    L = CHUNK
    gs, ks, es, bcums, blasts = [], [], [], [], []
    for qs, z, v16, lb, dmat, masks, reverse in chains:
        sig = _sigmoid(z)
        f = lb + (1.0 - lb) * sig
        g = jnp.log(jnp.maximum(f, F_MIN))
        g_hi, g_lo = _split_bf16(g)
        e2 = jnp.dot(dmat, jnp.concatenate([g_hi, g_lo], axis=1), preferred_element_type=F32)
        e = e2[:, :LANES] + e2[:, LANES:]
        gs.append(g)
        ks.append((1.0 - lb) * (1.0 - sig))
        es.append(e)
        bcums.append(e[0:L])
        blasts.append(e[0:1] if reverse else e[L - 1:L])

    scores = [jnp.zeros((L, L), F32) for _ in chains]
    for i, h in enumerate(LEVELS):
        for c, (qs, z, v16, lb, dmat, masks, reverse) in enumerate(chains):
            k = ks[c]
            if h <= 1:
                if h == 0:
                    prod = qs * k
                else:
                    prod = qs * jnp.exp(gs[c]) * pltpu.roll(k, L - 1 if reverse else 1, axis=0)
                scores[c] = scores[c] + masks[i] * jnp.sum(prod, axis=-1, keepdims=True)
                continue
            if h in MATMUL_LEVELS:
                m = 1 + 2 * MATMUL_LEVELS.index(h)
                qt = qs * jnp.exp(es[c][m * L:(m + 1) * L])
                kt = k * jnp.exp(es[c][(m + 1) * L:(m + 2) * L])
            else:
                eq, ek = _half_block_exponents(bcums[c], h, reverse)
                qt = qs * jnp.exp(eq)
                kt = k * jnp.exp(ek)
            s_h = lax.dot_general(qt.astype(BF16), kt.astype(BF16), (((1,), (1,)), ((), ())),
                                  preferred_element_type=F32)
            scores[c] = scores[c] + masks[i] * s_h

    outs = []
    for c, (qs, z, v16, lb, dmat, masks, reverse) in enumerate(chains):
        o = jnp.dot(scores[c].astype(BF16), v16, preferred_element_type=F32)
        qhat = (qs * jnp.exp(bcums[c])).astype(BF16)
        kvt = lax.dot_general(v16, (ks[c] * jnp.exp(blasts[c] - bcums[c])).astype(BF16),
                              (((0,), (0,)), ((), ())), preferred_element_type=F32)
        outs.append((o, qhat, kvt, blasts[c]))
    return outs


def _hgrn_kernel(ql, zfl, zbl, vl, ogl, qc, zfc, zbc, vc, ogc, lb_ref, ng_ref, dm_ref, mk_ref,
                 yl_ref, yc_ref, ol_s, oc_s, qh_s, kv_s, bl_s, st_s):
    lbf = lb_ref[0:1, :]
    lbb = lb_ref[1:2, :]

    def run(q_ref, zf_ref, zb_ref, v_ref, o_s, states):
        n = q_ref.shape[0] // CHUNK
        per_it = math.gcd(n, INTRA_CHUNKS)

        def intra(it, carry):
            chains = []
            for u in range(per_it):
                c = it * per_it + u
                r = pl.multiple_of(c * CHUNK, CHUNK)
                qs = _silu(q_ref[pl.ds(r, CHUNK), :]) * (HG_DK ** -0.5)
                v16 = v_ref[pl.ds(r, CHUNK), :].astype(BF16)
                chains.append((qs, zf_ref[pl.ds(r, CHUNK), :], v16, lbf, dm_ref[0], mk_ref[0], False))
                chains.append((qs, zb_ref[pl.ds(r, CHUNK), :], v16, lbb, dm_ref[1], mk_ref[1], True))
            res = _hgrn_intra(chains)
            for u in range(per_it):
                c = it * per_it + u
                r = pl.multiple_of(c * CHUNK, CHUNK)
                (of, qf, kvf, blf), (ob, qb, kvb, blb) = res[2 * u], res[2 * u + 1]
                o_s[pl.ds(r, CHUNK), :] = of + ob
                qh_s[pl.ds(r, CHUNK), 0:HG_DK] = qf
                qh_s[pl.ds(r, CHUNK), HG_DK:2 * HG_DK] = qb
                kv_s[0, c] = kvf
                kv_s[1, c] = kvb
                bl_s[0, c] = jnp.broadcast_to(blf, (SUBLANES, HG_DK))
                bl_s[1, c] = jnp.broadcast_to(blb, (SUBLANES, HG_DK))
            return carry

        lax.fori_loop(0, n // per_it, intra, 0)

        def recur(c, carry):
            sf, sb = carry
            cb = n - 1 - c
            st_s[c, :, 0:HG_DK] = sf.astype(BF16)
            st_s[cb, :, HG_DK:2 * HG_DK] = sb.astype(BF16)
            sf = sf * jnp.exp(bl_s[0, c][0:1, :]) + kv_s[0, c]
            sb = sb * jnp.exp(bl_s[1, cb][0:1, :]) + kv_s[1, cb]
            return sf, sb

        states = lax.fori_loop(0, n, recur, states)

        def inter(c, carry):
            r = pl.multiple_of(c * CHUNK, CHUNK)
            o_s[pl.ds(r, CHUNK), :] += lax.dot_general(
                qh_s[pl.ds(r, CHUNK), :], st_s[c], (((1,), (1,)), ((), ())), preferred_element_type=F32)
            return carry

        lax.fori_loop(0, n, inter, 0, unroll=4)
        return states

    zero = jnp.zeros((HG_DV, HG_DK), F32)
    states = run(qc, zfc, zbc, vc, oc_s, (zero, zero))
    run(ql, zfl, zbl, vl, ol_s, states)

    def readout(o_s, og_ref, y_ref):
        o = o_s[...]
        o = o * lax.rsqrt(jnp.mean(o * o, axis=-1, keepdims=True) + EPS) * ng_ref[...]
        y_ref[...] = (o * _silu(og_ref[...])).astype(y_ref.dtype)

    readout(ol_s, ogl, yl_ref)
    readout(oc_s, ogc, yc_ref)


def _hgrn_call(p_lat, p_ctx, lb, norm_g, batch, d_model):
    n_lat = p_lat.shape[0]
    n_ctx = p_ctx.shape[0]
    s_len = n_lat // batch
    c_len = n_ctx // batch
    d_half = d_model // 2
    heads = d_half // HG_DK
    cb = d_half // LANES
    q0, zf0, zb0, v0, og0 = 3 * cb, 4 * cb, 5 * cb, 6 * cb, 7 * cb

    def spec(rows, c0):
        return pl.BlockSpec((rows, LANES), lambda b, h, c0=c0: (b, c0 + h))

    dmat = jnp.asarray(np.stack([_decay_matrices(False), _decay_matrices(True)]), BF16)
    masks = jnp.asarray(np.stack([_level_masks(False), _level_masks(True)]), F32)
    full = lambda shape: pl.BlockSpec(shape, lambda b, h: (0,) * len(shape))

    return pl.pallas_call(
        _hgrn_kernel,
        grid=(batch, heads),
        in_specs=[spec(s_len, c0) for c0 in (q0, zf0, zb0, v0, og0)]
        + [spec(c_len, c0) for c0 in (q0, zf0, zb0, v0, og0)]
        + [pl.BlockSpec((2, LANES), lambda b, h: (0, h)),
           full((1, HG_DV)), full(dmat.shape), full(masks.shape)],
        out_specs=[pl.BlockSpec((s_len, LANES), lambda b, h: (b, h)),
                   pl.BlockSpec((c_len, LANES), lambda b, h: (b, h))],
        out_shape=[jax.ShapeDtypeStruct((n_lat, d_half), BF16),
                   jax.ShapeDtypeStruct((n_ctx, d_half), BF16)],
        scratch_shapes=[pltpu.VMEM((s_len, HG_DV), F32), pltpu.VMEM((c_len, HG_DV), F32),
                        pltpu.VMEM((s_len, 2 * HG_DK), BF16),
                        pltpu.VMEM((2, s_len // CHUNK, HG_DV, HG_DK), F32),
                        pltpu.VMEM((2, s_len // CHUNK, SUBLANES, HG_DK), F32),
                        pltpu.VMEM((s_len // CHUNK, HG_DV, 2 * HG_DK), BF16)],
        compiler_params=_cparams(("parallel", "parallel")),
        name="hgrn",
    )(*([p_lat] * 5), *([p_ctx] * 5), lb, norm_g.reshape(1, HG_DV), dmat, masks)


def _merge_kernel(cb_ref, cc_ref, cv_ref, ga_ref, gb_ref, yb_ref, x_ref, g1_ref, cw_ref, cbias_ref,
                  wpa_ref, wpb_ref, wo_ref, lng_ref, lnb_ref, o_ref, *, row_len, alpha):
    tm = x_ref.shape[0]
    u = cc_ref[...] * cv_ref[...]
    pos = lax.broadcasted_iota(jnp.int32, (tm, 1), 0) % row_len
    prev = jnp.where(pos == 0, 0.0, pltpu.roll(u, 1, axis=0))
    nxt = jnp.where(pos == row_len - 1, 0.0, pltpu.roll(u, tm - 1, axis=0))
    conv = prev * cw_ref[0:1, :] + u * cw_ref[1:2, :] + nxt * cw_ref[2:3, :] + cbias_ref[...]
    ya = (cb_ref[...] * conv).astype(BF16)
    m = (_sigmoid(ga_ref[...]) * jnp.dot(ya, wpa_ref[...], preferred_element_type=F32)
         + _sigmoid(gb_ref[...]) * jnp.dot(yb_ref[...], wpb_ref[...], preferred_element_type=F32))
    y = jnp.dot(m.astype(BF16), wo_ref[...], preferred_element_type=F32)
    r = alpha * x_ref[...] + g1_ref[0] * y
    o_ref[...] = _ln_rows(r) * lng_ref[...] + lnb_ref[...]


def _merge_call(p, yb, x2d, mods3, conv_w, conv_b, wpa, wpb, wo, ln_g, ln_b, rows_per_mod, mod_row0,
                row_len, tm, alpha):
    n, d_model = x2d.shape
    tm = _tile(rows_per_mod or n, tm)
    d_half = d_model // 2

    if rows_per_mod is None:
        g1_idx = lambda i: (mod_row0 * 6 + 2, 0, 0)
    else:
        g1_idx = lambda i: ((mod_row0 + (i * tm) // rows_per_mod) * 6 + 2, 0, 0)
    const = lambda shape: pl.BlockSpec(shape, lambda i: (0,) * len(shape), pipeline_mode=pl.Buffered(1))
    return pl.pallas_call(
        functools.partial(_merge_kernel, row_len=row_len, alpha=alpha),
        grid=(n // tm,),
        in_specs=[
            pl.BlockSpec((tm, d_half), lambda i: (i, 0)),
            pl.BlockSpec((tm, d_half), lambda i: (i, 1)),
            pl.BlockSpec((tm, d_half), lambda i: (i, 2)),
            pl.BlockSpec((tm, d_model), lambda i: (i, 4)),
            pl.BlockSpec((tm, d_model), lambda i: (i, 5)),
            pl.BlockSpec((tm, d_half), lambda i: (i, 0)),
            pl.BlockSpec((tm, d_model), lambda i: (i, 0)),
            pl.BlockSpec((1, 1, d_model), g1_idx),
            const((3, d_half)), const((1, d_half)),
            const((d_half, d_model)), const((d_half, d_model)), const((d_model, d_model)),
            const((1, d_model)), const((1, d_model)),
        ],
        out_specs=pl.BlockSpec((tm, d_model), lambda i: (i, 0)),
        out_shape=jax.ShapeDtypeStruct((n, d_model), F32),
        compiler_params=_cparams(("parallel",)),
        name="merge",
    )(p, p, p, p, p, yb, x2d, mods3, conv_w, conv_b.reshape(1, d_half), wpa, wpb, wo,
      ln_g.reshape(1, d_model), ln_b.reshape(1, d_model))


def _topk_rows(s, k, payload=None):
    rows = s.shape[0]
    riota = lax.broadcasted_iota(jnp.int32, s.shape, 0).astype(F32)
    vals, picks = [], []
    for _ in range(k):
        m = jnp.max(s, axis=0, keepdims=True)
        am = jnp.min(jnp.where(s == m, riota, float(rows)), axis=0, keepdims=True)
        sel = riota == am
        vals.append(m)
        if payload is None:
            picks.append(am)
        else:
            picks.append(jnp.max(jnp.where(sel, payload, -1.0), axis=0, keepdims=True))
        s = jnp.where(sel, -jnp.inf, s)
    return jnp.concatenate(vals, axis=0), jnp.concatenate(picks, axis=0)


def _pair_candidates():
    return [(a, min(TOPK, TOPK // (a + 1))) for a in range(TOPK)]


def _route_kernel(x_ref, sh_ref, sc_ref, wq_ref, keys_ref, h_ref, ii_ref, jj_ref, gg_ref):
    hmod = _ln_rows(x_ref[...]) * (1.0 + sc_ref[0]) + sh_ref[0]
    h16 = hmod.astype(BF16)
    h_ref[...] = h16
    qp = jnp.dot(h16, wq_ref[...], preferred_element_type=F32).astype(BF16)
    i_rows, j_rows, g_rows = [], [], []
    for head in range(PEER_HEADS):
        sv, si = [], []
        for half in range(2):
            c = (head * 2 + half) * LANES
            s = lax.dot_general(keys_ref[head * 2 + half], qp[:, c:c + LANES], (((1,), (1,)), ((), ())),
                                preferred_element_type=F32)
            v_, i_ = _topk_rows(s, TOPK)
            sv.append(v_)
            si.append(i_)
        pairs = _pair_candidates()
        n_pad = -sum(nb for _, nb in pairs) % SUBLANES
        tm = s.shape[1]
        cand = jnp.concatenate([sv[0][a:a + 1, :] + sv[1][0:nb, :] for a, nb in pairs]
                               + [jnp.full((n_pad, tm), -jnp.inf, F32)], axis=0)
        cidx = jnp.concatenate([si[0][a:a + 1, :] * float(N_KEYS) + si[1][0:nb, :] for a, nb in pairs]
                               + [jnp.zeros((n_pad, tm), F32)], axis=0)
        tv, te = _topk_rows(cand, TOPK, payload=cidx)
        te = te.astype(jnp.int32)
        ex = jnp.exp(tv - tv[0:1, :])
        g_rows.append(ex / jnp.sum(ex, axis=0, keepdims=True))
        i_rows.append(lax.shift_right_logical(te, 7))
        j_rows.append(te & (N_KEYS - 1))
    ii_ref[...] = jnp.concatenate(i_rows, axis=0).T
    jj_ref[...] = jnp.concatenate(j_rows, axis=0).T
    gg_ref[...] = jnp.concatenate(g_rows, axis=0).T


def _route_call(x2d, mods3, wq, keys, rows_per_mod, mod_row0, tm):
    n, d_model = x2d.shape
    tm = _tile(rows_per_mod or n, tm)
    slots = PEER_HEADS * TOPK

    def mod_idx(g):
        if rows_per_mod is None:
            return lambda i: (mod_row0 * 6 + g, 0, 0)
        return lambda i: ((mod_row0 + (i * tm) // rows_per_mod) * 6 + g, 0, 0)

    const = lambda shape: pl.BlockSpec(shape, lambda i: (0,) * len(shape), pipeline_mode=pl.Buffered(1))
    return pl.pallas_call(
        _route_kernel,
        grid=(n // tm,),
        in_specs=[
            pl.BlockSpec((tm, d_model), lambda i: (i, 0)),
            pl.BlockSpec((1, 1, d_model), mod_idx(3)),
            pl.BlockSpec((1, 1, d_model), mod_idx(4)),
            const(wq.shape), const(keys.shape),
        ],
        out_specs=[pl.BlockSpec((tm, d_model), lambda i: (i, 0))]
        + [pl.BlockSpec((tm, slots), lambda i: (i, 0))] * 3,
        out_shape=[jax.ShapeDtypeStruct((n, d_model), BF16),
                   jax.ShapeDtypeStruct((n, slots), jnp.int32),
                   jax.ShapeDtypeStruct((n, slots), jnp.int32),
                   jax.ShapeDtypeStruct((n, slots), F32)],
        compiler_params=_cparams(("parallel",)),
        name="route",
    )(x2d, mods3, mods3, wq, keys)


def _gelu_exact(x):
    return 0.5 * x * (1.0 + lax.erf(x * (2.0 ** -0.5)))


def _gates_kernel(ii_ref, jj_ref, gg_ref, o_ref, gate_s):
    tm = ii_ref.shape[0]
    sub = lax.broadcasted_iota(jnp.int32, (N_KEYS, N_KEYS), 0)

    def tokens(it, carry):
        base = pl.multiple_of(it * (GATE_UNROLL * G_PITCH), SUBLANES)
        for u in range(GATE_UNROLL):
            n = it * GATE_UNROLL + u
            ii = ii_ref[pl.ds(n, 1), :]
            jj = jj_ref[pl.ds(n, 1), :]
            gg = gg_ref[pl.ds(n, 1), :]
            a_t = jnp.where(ii == sub, gg, 0.0).astype(BF16)
            b = jnp.where(jj == sub, 1.0, 0.0).T.astype(BF16)
            gate = jnp.dot(a_t, b, preferred_element_type=F32)
            gate_s[pl.ds(base + u * G_PITCH, N_KEYS), :] = gate
        return carry

    lax.fori_loop(0, tm // GATE_UNROLL, tokens, 0)
    for i in range(N_KEYS):
        o_ref[:, i * N_KEYS:(i + 1) * N_KEYS] = gate_s[pl.ds(i, tm, stride=G_PITCH), :].astype(o_ref.dtype)


def _gates_call(ii, jj, gg, tm):
    n, slots = ii.shape
    tm = _tile(n, tm)
    tok = pl.BlockSpec((tm, slots), lambda i: (i, 0))
    return pl.pallas_call(
        _gates_kernel,
        grid=(n // tm,),
        in_specs=[tok, tok, tok],
        out_specs=pl.BlockSpec((tm, N_KEYS * N_KEYS), lambda i: (i, 0)),
        out_shape=jax.ShapeDtypeStruct((n, N_KEYS * N_KEYS), BF16),
        scratch_shapes=[pltpu.VMEM((tm * G_PITCH, LANES), F32)],
        compiler_params=_cparams(("parallel",)),
        name="gates",
    )(ii, jj, gg)


def _dense_kernel(h_ref, gate_ref, ut_ref, v_ref, x_ref, g2_ref, lng_ref, lnb_ref, o_ref, *, alpha):
    eb = pl.program_id(1)

    @pl.when(eb == 0)
    def _():
        o_ref[...] = jnp.zeros(o_ref.shape, F32)

    act = _gelu_exact(jnp.dot(h_ref[...], ut_ref[0, 0], preferred_element_type=F32))
    hg = (gate_ref[...].astype(F32) * act).astype(BF16)
    o_ref[...] += jnp.dot(hg, v_ref[0], preferred_element_type=F32)

    @pl.when(eb == pl.num_programs(1) - 1)
    def _():
        r = alpha * x_ref[...] + g2_ref[0] * o_ref[...]
        o_ref[...] = _ln_rows(r) * lng_ref[...] + lnb_ref[...]


def _dense_call(h16, gates, ut4, v16, layer, x2d, mods3, ln_g, ln_b, rows_per_mod, mod_row0, tm, alpha):
    n, d_model = x2d.shape
    tm = _tile(rows_per_mod or n, tm)
    _, n_blocks, _, te = ut4.shape

    if rows_per_mod is None:
        g2_idx = lambda i, e: (mod_row0 * 6 + 5, 0, 0)
    else:
        g2_idx = lambda i, e: ((mod_row0 + (i * tm) // rows_per_mod) * 6 + 5, 0, 0)
    tok = lambda cols: pl.BlockSpec((tm, cols), lambda i, e: (i, 0))
    const = lambda shape: pl.BlockSpec(shape, lambda i, e: (0,) * len(shape))
    return pl.pallas_call(
        functools.partial(_dense_kernel, alpha=alpha),
        grid=(n // tm, n_blocks),
        in_specs=[
            tok(d_model),
            pl.BlockSpec((tm, te), lambda i, e: (i, e)),
            pl.BlockSpec((1, 1, d_model, te), lambda i, e: (layer, e, 0, 0)),
            pl.BlockSpec((1, te, d_model), lambda i, e: (layer, e, 0)),
            tok(d_model),
            pl.BlockSpec((1, 1, d_model), g2_idx),
            const((1, d_model)), const((1, d_model)),
        ],
        out_specs=tok(d_model),
        out_shape=jax.ShapeDtypeStruct((n, d_model), F32),
        compiler_params=_cparams(("parallel", "arbitrary")),
        name="dense",
    )(h16, gates, ut4, v16, x2d, mods3, ln_g.reshape(1, d_model), ln_b.reshape(1, d_model))


def kernel(x, c, ctx, c_ctx, w_mod, b_mod, w_in, conv_w, conv_b, lb_raw, hg_norm_g, w_pa, w_pb, w_o,
           ln1_g, ln1_b, peer_wq, peer_keys, peer_u, peer_v, ln2_g, ln2_b):
    batch, s_len, d_model = x.shape
    c_len = ctx.shape[1]
    depth = w_mod.shape[0]
    alpha = (2.0 * depth) ** 0.25
    n_lat = batch * s_len
    n_ctx = batch * c_len

    p = jax.nn.softmax(lb_raw.astype(F32), axis=0)
    lower = jnp.cumsum(p, axis=0) - p[:1]

    cond = jnp.concatenate([c, c_ctx[None, :]], axis=0)
    cb = jnp.broadcast_to(cond.T[:, :, None], (d_model, batch + 1, LANES)).reshape(d_model, -1)
    mods = _mod_call(cb, w_mod, b_mod, batch + 1)

    w_in16 = _cast_call(w_in)
    v16 = _cast_call(peer_v)
    ut4 = _cast_t_call(peer_u, DENSE_TE)

    xl = x.reshape(n_lat, d_model)
    xc = ctx.reshape(n_ctx, d_model)
    for l in range(depth):
        last = l == depth - 1
        mods3 = mods[l].reshape(16 * 6, 1, d_model)
        wpa, wpb, wo = w_pa[l].astype(BF16), w_pb[l].astype(BF16), w_o[l].astype(BF16)
        wq = peer_wq[l].astype(BF16)
        keys = peer_keys[l].reshape(PEER_HEADS * 2, N_KEYS, -1).astype(BF16)

        p_lat = _inproj_call(xl, mods3, w_in16, l, s_len, 0, 1024)
        p_ctx = _inproj_call(xc, mods3, w_in16, l, None, batch, 1024)
        yb_lat, yb_ctx = _hgrn_call(p_lat, p_ctx, lower[l], hg_norm_g[l], batch, d_model)

        streams = [(xl, p_lat, yb_lat, s_len, 0, GRID_W)]
        if not last:
            streams.append((xc, p_ctx, yb_ctx, None, batch, c_len))
        outs = []
        for xs, ps, ybs, rpm, row0, row_len in streams:
            x1 = _merge_call(ps, ybs, xs, mods3, conv_w[l], conv_b[l], wpa, wpb, wo, ln1_g[l], ln1_b[l],
                             rpm, row0, row_len, 256, alpha)
            h16, ii, jj, gg = _route_call(x1, mods3, wq, keys, rpm, row0, 256)
            gates = _gates_call(ii, jj, gg, 256)
            outs.append(_dense_call(h16, gates, ut4, v16, l, x1, mods3, ln2_g[l], ln2_b[l],
                                    rpm, row0, DENSE_TM, alpha))
        xl = outs[0]
        if not last:
            xc = outs[1]
    return xl.reshape(batch, s_len, d_model)
```

```python
import functools
import math

import numpy as np
import jax
import jax.numpy as jnp
from jax import lax
from jax.experimental import pallas as pl
from jax.experimental.pallas import tpu as pltpu

F32 = jnp.float32
BF16 = jnp.bfloat16

LANES = 128
SUBLANES = 8
VMEM_LIMIT = 56 * 1024 * 1024

EPS = 1e-6
F_MIN = 1e-30
GRID_W = 64
HG_DK = 128
HG_DV = 128
CHUNK = 64
LEVELS = (32, 16, 8, 4, 2, 1, 0)
MATMUL_LEVELS = (4, 2)
INTRA_CHUNKS = 4
N_KEYS = 128
TOPK = 16
PEER_HEADS = 8
G_PITCH = 132
GATE_UNROLL = 64
DENSE_TE = 512
DENSE_TM = 1024
DENSE_VMEM_LIMIT = 62 * 1024 * 1024
CAST_BLOCK_BYTES = 4 * 1024 * 1024


def _cparams(sem, vmem_limit=VMEM_LIMIT):
    return pltpu.CompilerParams(dimension_semantics=sem, vmem_limit_bytes=vmem_limit)


def _tile(n, pref):
    t = min(n, pref)
    while n % t:
        t //= 2
    return t


def _ln_rows(x):
    mu = jnp.mean(x, axis=-1, keepdims=True)
    xc = x - mu
    var = jnp.mean(xc * xc, axis=-1, keepdims=True)
    return xc * lax.rsqrt(var + EPS)


def _sigmoid(x):
    return 1.0 / (1.0 + jnp.exp(-x))


def _silu(x):
    return x * _sigmoid(x)


def _split_bf16(x):
    hi = x.astype(BF16)
    lo = (x - hi.astype(F32)).astype(BF16)
    return hi, lo


def _cast_kernel(x_ref, o_ref):
    o_ref[...] = x_ref[...].astype(o_ref.dtype)


def _cast_call(w):
    depth, r, c = w.shape
    rows = _tile(r, 1 << max(4, (CAST_BLOCK_BYTES // (4 * c)).bit_length() - 1))
    spec = pl.BlockSpec((1, rows, c), lambda l, i: (l, i, 0))
    return pl.pallas_call(
        _cast_kernel, grid=(depth, r // rows), in_specs=[spec], out_specs=spec,
        out_shape=jax.ShapeDtypeStruct(w.shape, BF16),
        compiler_params=_cparams(("parallel", "parallel")), name="cast",
    )(w)


def _cast_t_kernel(x_ref, o_ref):
    d = x_ref.shape[2]
    step = 256 if d % 256 == 0 else d
    for c in range(0, d, step):
        o_ref[0, 0, c:c + step, :] = x_ref[0, :, c:c + step].T.astype(o_ref.dtype)


def _cast_t_call(w, te):
    depth, e, d = w.shape
    return pl.pallas_call(
        _cast_t_kernel, grid=(depth, e // te),
        in_specs=[pl.BlockSpec((1, te, d), lambda l, i: (l, i, 0))],
        out_specs=pl.BlockSpec((1, 1, d, te), lambda l, i: (l, i, 0, 0)),
        out_shape=jax.ShapeDtypeStruct((depth, e // te, d, te), BF16),
        compiler_params=_cparams(("parallel", "parallel")), name="cast_t",
    )(w)


def _mod_kernel(cb_ref, w_ref, b_ref, o_ref, s_tab, *, n_rows):
    d_model = w_ref.shape[1]
    tn = w_ref.shape[2]

    @pl.when((pl.program_id(0) == 0) & (pl.program_id(1) == 0))
    def _():
        def fill(g, carry):
            r0 = pl.multiple_of(g * 64, 64)
            s_tab[pl.ds(r0, 64), :] = _silu(cb_ref[pl.ds(r0, 64), :])
            return carry

        lax.fori_loop(0, d_model // 64, fill, 0)

    def body(g, accs):
        r0 = pl.multiple_of(g * SUBLANES, SUBLANES)
        w8 = w_ref[0, pl.ds(r0, SUBLANES), :]
        out = []
        for r, acc in enumerate(accs):
            s8 = s_tab[pl.ds(r0, SUBLANES), r * LANES:(r + 1) * LANES]
            out.append(acc + jnp.concatenate([s8] * (tn // LANES), axis=1) * w8)
        return tuple(out)

    init = tuple(jnp.zeros((SUBLANES, tn), F32) for _ in range(n_rows))
    accs = lax.fori_loop(0, d_model // SUBLANES, body, init, unroll=2)
    bias = b_ref[0]
    o_ref[0] = jnp.zeros((o_ref.shape[1], tn), F32)
    for r, acc in enumerate(accs):
        o_ref[0, r:r + 1, :] = jnp.sum(acc, axis=0, keepdims=True) + bias


def _mod_call(cb, w_mod, b_mod, n_rows):
    depth, d_model, width = w_mod.shape
    tn = 512
    return pl.pallas_call(
        functools.partial(_mod_kernel, n_rows=n_rows),
        grid=(depth, width // tn),
        in_specs=[
            pl.BlockSpec((d_model, n_rows * LANES), lambda l, j: (0, 0), pipeline_mode=pl.Buffered(1)),
            pl.BlockSpec((1, d_model, tn), lambda l, j: (l, 0, j)),
            pl.BlockSpec((1, 1, tn), lambda l, j: (l, 0, j)),
        ],
        out_specs=pl.BlockSpec((1, 16, tn), lambda l, j: (l, 0, j)),
        out_shape=jax.ShapeDtypeStruct((depth, 16, width), F32),
        scratch_shapes=[pltpu.VMEM((d_model, n_rows * LANES), F32)],
        compiler_params=_cparams(("arbitrary", "arbitrary")),
        name="mod",
    )(cb, w_mod, b_mod.reshape(depth, 1, width))


def _inproj_kernel(x_ref, sh_ref, sc_ref, w_ref, o_ref, h_ref):
    @pl.when(pl.program_id(1) == 0)
    def _():
        y = _ln_rows(x_ref[...])
        h_ref[...] = (y * (1.0 + sc_ref[0]) + sh_ref[0]).astype(BF16)

    o_ref[...] = jnp.dot(h_ref[...], w_ref[0], preferred_element_type=F32)


def _inproj_call(x2d, mods3, w_bf16, layer, rows_per_mod, mod_row0, tm):
    n, d_model = x2d.shape
    tm = _tile(rows_per_mod or n, tm)
    width = w_bf16.shape[2]
    tn = _tile(width, 1024)

    def mod_idx(g):
        if rows_per_mod is None:
            return lambda i, j: (mod_row0 * 6 + g, 0, 0)
        return lambda i, j: ((mod_row0 + (i * tm) // rows_per_mod) * 6 + g, 0, 0)

    return pl.pallas_call(
        _inproj_kernel,
        grid=(n // tm, width // tn),
        in_specs=[
            pl.BlockSpec((tm, d_model), lambda i, j: (i, 0)),
            pl.BlockSpec((1, 1, d_model), mod_idx(0)),
            pl.BlockSpec((1, 1, d_model), mod_idx(1)),
            pl.BlockSpec((1, d_model, tn), lambda i, j: (layer, 0, j)),
        ],
        out_specs=pl.BlockSpec((tm, tn), lambda i, j: (i, j)),
        out_shape=jax.ShapeDtypeStruct((n, width), F32),
        scratch_shapes=[pltpu.VMEM((tm, d_model), BF16)],
        compiler_params=_cparams(("parallel", "arbitrary")),
        name="inproj",
    )(x2d, mods3, mods3, w_bf16)


def _decay_matrices(reverse):
    L = CHUNK
    rank = np.arange(L)[::-1] if reverse else np.arange(L)
    rt = rank[:, None]
    rr = rank[None, :]
    blocks = [(rr <= rt)]
    for h in MATMUL_LEVELS:
        upper = (rt & h) != 0
        blocks.append(upper & (rr >= (rt & ~(h - 1))) & (rr <= rt))
        blocks.append((~upper) & (rr > rt) & (rr <= (rt | (h - 1))))
    return np.concatenate(blocks, axis=0).astype(np.float32)


def _half_block_exponents(bcum, h, reverse):
    eq, ek = [], []
    zero = jnp.zeros((h, LANES), F32)
    for b in range(CHUNK // h):
        rows = bcum[b * h:(b + 1) * h]
        first_of_pair = b % 2 == 0
        if not reverse:
            if first_of_pair:
                eq.append(zero)
                ek.append(bcum[(b + 1) * h - 1:(b + 1) * h] - rows)
            else:
                eq.append(rows - bcum[b * h - 1:b * h])
                ek.append(zero)
        else:
            if first_of_pair:
                eq.append(rows - bcum[(b + 1) * h:(b + 1) * h + 1])
                ek.append(zero)
            else:
                eq.append(zero)
                ek.append(bcum[b * h:b * h + 1] - rows)
    return jnp.concatenate(eq, axis=0), jnp.concatenate(ek, axis=0)


def _level_masks(reverse):
    L = CHUNK
    rank = np.arange(L)[::-1] if reverse else np.arange(L)
    rt = rank[:, None]
    rs = rank[None, :]
    out = []
    for h in LEVELS:
        if h == 0:
            out.append(rt == rs)
        else:
            out.append(((rt ^ rs) // h == 1) & (rt > rs))
    return np.stack(out).astype(np.float32)


def _hgrn_intra(chains):
    L = CHUNK
    gs, ks, es, bcums, blasts = [], [], [], [], []
    for qs, z, v16, lb, dmat, masks, reverse in chains:
        sig = _sigmoid(z)
        f = lb + (1.0 - lb) * sig
        g = jnp.log(jnp.maximum(f, F_MIN))
        g_hi, g_lo = _split_bf16(g)
        e2 = jnp.dot(dmat, jnp.concatenate([g_hi, g_lo], axis=1), preferred_element_type=F32)
        e = e2[:, :LANES] + e2[:, LANES:]
        gs.append(g)
        ks.append((1.0 - lb) * (1.0 - sig))
        es.append(e)
        bcums.append(e[0:L])
        blasts.append(e[0:1] if reverse else e[L - 1:L])

    scores = [jnp.zeros((L, L), F32) for _ in chains]
    for i, h in enumerate(LEVELS):
        for c, (qs, z, v16, lb, dmat, masks, reverse) in enumerate(chains):
            k = ks[c]
            if h <= 1:
                if h == 0:
                    prod = qs * k
                else:
                    prod = qs * jnp.exp(gs[c]) * pltpu.roll(k, L - 1 if reverse else 1, axis=0)
                scores[c] = scores[c] + masks[i] * jnp.sum(prod, axis=-1, keepdims=True)
                continue
            if h in MATMUL_LEVELS:
                m = 1 + 2 * MATMUL_LEVELS.index(h)
                qt = qs * jnp.exp(es[c][m * L:(m + 1) * L])
                kt = k * jnp.exp(es[c][(m + 1) * L:(m + 2) * L])
            else:
                eq, ek = _half_block_exponents(bcums[c], h, reverse)
                qt = qs * jnp.exp(eq)
                kt = k * jnp.exp(ek)
            s_h = lax.dot_general(qt.astype(BF16), kt.astype(BF16), (((1,), (1,)), ((), ())),
                                  preferred_element_type=F32)
            scores[c] = scores[c] + masks[i] * s_h

    outs = []
    for c, (qs, z, v16, lb, dmat, masks, reverse) in enumerate(chains):
        o = jnp.dot(scores[c].astype(BF16), v16, preferred_element_type=F32)
        qhat = (qs * jnp.exp(bcums[c])).astype(BF16)
        kvt = lax.dot_general(v16, (ks[c] * jnp.exp(blasts[c] - bcums[c])).astype(BF16),
                              (((0,), (0,)), ((), ())), preferred_element_type=F32)
        outs.append((o, qhat, kvt, blasts[c]))
    return outs


def _hgrn_kernel(ql, zfl, zbl, vl, ogl, qc, zfc, zbc, vc, ogc, lb_ref, ng_ref, dm_ref, mk_ref,
                 yl_ref, yc_ref, ol_s, oc_s, qh_s, kv_s, bl_s, st_s):
    lbf = lb_ref[0:1, :]
    lbb = lb_ref[1:2, :]

    def run(q_ref, zf_ref, zb_ref, v_ref, o_s, states):
        n = q_ref.shape[0] // CHUNK
        per_it = math.gcd(n, INTRA_CHUNKS)

        def intra(it, carry):
            chains = []
            for u in range(per_it):
                c = it * per_it + u
                r = pl.multiple_of(c * CHUNK, CHUNK)
                qs = _silu(q_ref[pl.ds(r, CHUNK), :]) * (HG_DK ** -0.5)
                v16 = v_ref[pl.ds(r, CHUNK), :].astype(BF16)
                chains.append((qs, zf_ref[pl.ds(r, CHUNK), :], v16, lbf, dm_ref[0], mk_ref[0], False))
                chains.append((qs, zb_ref[pl.ds(r, CHUNK), :], v16, lbb, dm_ref[1], mk_ref[1], True))
            res = _hgrn_intra(chains)
            for u in range(per_it):
                c = it * per_it + u
                r = pl.multiple_of(c * CHUNK, CHUNK)
                (of, qf, kvf, blf), (ob, qb, kvb, blb) = res[2 * u], res[2 * u + 1]
                o_s[pl.ds(r, CHUNK), :] = of + ob
                qh_s[pl.ds(r, CHUNK), 0:HG_DK] = qf
                qh_s[pl.ds(r, CHUNK), HG_DK:2 * HG_DK] = qb
                kv_s[0, c] = kvf
                kv_s[1, c] = kvb
                bl_s[0, c] = jnp.broadcast_to(blf, (SUBLANES, HG_DK))
                bl_s[1, c] = jnp.broadcast_to(blb, (SUBLANES, HG_DK))
            return carry

        lax.fori_loop(0, n // per_it, intra, 0)

        def recur(c, carry):
            sf, sb = carry
            cb = n - 1 - c
            st_s[c, :, 0:HG_DK] = sf.astype(BF16)
            st_s[cb, :, HG_DK:2 * HG_DK] = sb.astype(BF16)
            sf = sf * jnp.exp(bl_s[0, c][0:1, :]) + kv_s[0, c]
            sb = sb * jnp.exp(bl_s[1, cb][0:1, :]) + kv_s[1, cb]
            return sf, sb

        states = lax.fori_loop(0, n, recur, states)

        def inter(c, carry):
            r = pl.multiple_of(c * CHUNK, CHUNK)
            o_s[pl.ds(r, CHUNK), :] += lax.dot_general(
                qh_s[pl.ds(r, CHUNK), :], st_s[c], (((1,), (1,)), ((), ())), preferred_element_type=F32)
            return carry

        lax.fori_loop(0, n, inter, 0, unroll=4)
        return states

    zero = jnp.zeros((HG_DV, HG_DK), F32)
    states = run(qc, zfc, zbc, vc, oc_s, (zero, zero))
    run(ql, zfl, zbl, vl, ol_s, states)

    def readout(o_s, og_ref, y_ref):
        o = o_s[...]
        o = o * lax.rsqrt(jnp.mean(o * o, axis=-1, keepdims=True) + EPS) * ng_ref[...]
        y_ref[...] = (o * _silu(og_ref[...])).astype(y_ref.dtype)

    readout(ol_s, ogl, yl_ref)
    readout(oc_s, ogc, yc_ref)


def _hgrn_call(p_lat, p_ctx, lb, norm_g, batch, d_model):
    n_lat = p_lat.shape[0]
    n_ctx = p_ctx.shape[0]
    s_len = n_lat // batch
    c_len = n_ctx // batch
    d_half = d_model // 2
    heads = d_half // HG_DK
    cb = d_half // LANES
    q0, zf0, zb0, v0, og0 = 3 * cb, 4 * cb, 5 * cb, 6 * cb, 7 * cb

    def spec(rows, c0):
        return pl.BlockSpec((rows, LANES), lambda b, h, c0=c0: (b, c0 + h))

    dmat = jnp.asarray(np.stack([_decay_matrices(False), _decay_matrices(True)]), BF16)
    masks = jnp.asarray(np.stack([_level_masks(False), _level_masks(True)]), F32)
    full = lambda shape: pl.BlockSpec(shape, lambda b, h: (0,) * len(shape))

    return pl.pallas_call(
        _hgrn_kernel,
        grid=(batch, heads),
        in_specs=[spec(s_len, c0) for c0 in (q0, zf0, zb0, v0, og0)]
        + [spec(c_len, c0) for c0 in (q0, zf0, zb0, v0, og0)]
        + [pl.BlockSpec((2, LANES), lambda b, h: (0, h)),
           full((1, HG_DV)), full(dmat.shape), full(masks.shape)],
        out_specs=[pl.BlockSpec((s_len, LANES), lambda b, h: (b, h)),
                   pl.BlockSpec((c_len, LANES), lambda b, h: (b, h))],
        out_shape=[jax.ShapeDtypeStruct((n_lat, d_half), BF16),
                   jax.ShapeDtypeStruct((n_ctx, d_half), BF16)],
        scratch_shapes=[pltpu.VMEM((s_len, HG_DV), F32), pltpu.VMEM((c_len, HG_DV), F32),
                        pltpu.VMEM((s_len, 2 * HG_DK), BF16),
                        pltpu.VMEM((2, s_len // CHUNK, HG_DV, HG_DK), F32),
                        pltpu.VMEM((2, s_len // CHUNK, SUBLANES, HG_DK), F32),
                        pltpu.VMEM((s_len // CHUNK, HG_DV, 2 * HG_DK), BF16)],
        compiler_params=_cparams(("parallel", "parallel")),
        name="hgrn",
    )(*([p_lat] * 5), *([p_ctx] * 5), lb, norm_g.reshape(1, HG_DV), dmat, masks)


def _merge_kernel(cb_ref, cc_ref, cv_ref, ga_ref, gb_ref, yb_ref, x_ref, g1_ref, cw_ref, cbias_ref,
                  wpa_ref, wpb_ref, wo_ref, lng_ref, lnb_ref, o_ref, *, row_len, alpha):
    tm = x_ref.shape[0]
    u = cc_ref[...] * cv_ref[...]
    pos = lax.broadcasted_iota(jnp.int32, (tm, 1), 0) % row_len
    prev = jnp.where(pos == 0, 0.0, pltpu.roll(u, 1, axis=0))
    nxt = jnp.where(pos == row_len - 1, 0.0, pltpu.roll(u, tm - 1, axis=0))
    conv = prev * cw_ref[0:1, :] + u * cw_ref[1:2, :] + nxt * cw_ref[2:3, :] + cbias_ref[...]
    ya = (cb_ref[...] * conv).astype(BF16)
    m = (_sigmoid(ga_ref[...]) * jnp.dot(ya, wpa_ref[...], preferred_element_type=F32)
         + _sigmoid(gb_ref[...]) * jnp.dot(yb_ref[...], wpb_ref[...], preferred_element_type=F32))
    y = jnp.dot(m.astype(BF16), wo_ref[...], preferred_element_type=F32)
    r = alpha * x_ref[...] + g1_ref[0] * y
    o_ref[...] = _ln_rows(r) * lng_ref[...] + lnb_ref[...]


def _merge_call(p, yb, x2d, mods3, conv_w, conv_b, wpa, wpb, wo, ln_g, ln_b, rows_per_mod, mod_row0,
                row_len, tm, alpha):
    n, d_model = x2d.shape
    tm = _tile(rows_per_mod or n, tm)
    d_half = d_model // 2

    if rows_per_mod is None:
        g1_idx = lambda i: (mod_row0 * 6 + 2, 0, 0)
    else:
        g1_idx = lambda i: ((mod_row0 + (i * tm) // rows_per_mod) * 6 + 2, 0, 0)
    const = lambda shape: pl.BlockSpec(shape, lambda i: (0,) * len(shape), pipeline_mode=pl.Buffered(1))
    return pl.pallas_call(
        functools.partial(_merge_kernel, row_len=row_len, alpha=alpha),
        grid=(n // tm,),
        in_specs=[
            pl.BlockSpec((tm, d_half), lambda i: (i, 0)),
            pl.BlockSpec((tm, d_half), lambda i: (i, 1)),
            pl.BlockSpec((tm, d_half), lambda i: (i, 2)),
            pl.BlockSpec((tm, d_model), lambda i: (i, 4)),
            pl.BlockSpec((tm, d_model), lambda i: (i, 5)),
            pl.BlockSpec((tm, d_half), lambda i: (i, 0)),
            pl.BlockSpec((tm, d_model), lambda i: (i, 0)),
            pl.BlockSpec((1, 1, d_model), g1_idx),
            const((3, d_half)), const((1, d_half)),
            const((d_half, d_model)), const((d_half, d_model)), const((d_model, d_model)),
            const((1, d_model)), const((1, d_model)),
        ],
        out_specs=pl.BlockSpec((tm, d_model), lambda i: (i, 0)),
        out_shape=jax.ShapeDtypeStruct((n, d_model), F32),
        compiler_params=_cparams(("parallel",)),
        name="merge",
    )(p, p, p, p, p, yb, x2d, mods3, conv_w, conv_b.reshape(1, d_half), wpa, wpb, wo,
      ln_g.reshape(1, d_model), ln_b.reshape(1, d_model))


def _topk_rows(s, k, payload=None):
    rows = s.shape[0]
    riota = lax.broadcasted_iota(jnp.int32, s.shape, 0).astype(F32)
    vals, picks = [], []
    for _ in range(k):
        m = jnp.max(s, axis=0, keepdims=True)
        am = jnp.min(jnp.where(s == m, riota, float(rows)), axis=0, keepdims=True)
        sel = riota == am
        vals.append(m)
        if payload is None:
            picks.append(am)
        else:
            picks.append(jnp.max(jnp.where(sel, payload, -1.0), axis=0, keepdims=True))
        s = jnp.where(sel, -jnp.inf, s)
    return jnp.concatenate(vals, axis=0), jnp.concatenate(picks, axis=0)


def _pair_candidates():
    return [(a, min(TOPK, TOPK // (a + 1))) for a in range(TOPK)]


def _route_kernel(x_ref, sh_ref, sc_ref, wq_ref, keys_ref, h_ref, ii_ref, jj_ref, gg_ref):
    hmod = _ln_rows(x_ref[...]) * (1.0 + sc_ref[0]) + sh_ref[0]
    h16 = hmod.astype(BF16)
    h_ref[...] = h16
    qp = jnp.dot(h16, wq_ref[...], preferred_element_type=F32).astype(BF16)
    i_rows, j_rows, g_rows = [], [], []
    for head in range(PEER_HEADS):
        sv, si = [], []
        for half in range(2):
            c = (head * 2 + half) * LANES
            s = lax.dot_general(keys_ref[head * 2 + half], qp[:, c:c + LANES], (((1,), (1,)), ((), ())),
                                preferred_element_type=F32)
            v_, i_ = _topk_rows(s, TOPK)
            sv.append(v_)
            si.append(i_)
        pairs = _pair_candidates()
        n_pad = -sum(nb for _, nb in pairs) % SUBLANES
        tm = s.shape[1]
        cand = jnp.concatenate([sv[0][a:a + 1, :] + sv[1][0:nb, :] for a, nb in pairs]
                               + [jnp.full((n_pad, tm), -jnp.inf, F32)], axis=0)
        cidx = jnp.concatenate([si[0][a:a + 1, :] * float(N_KEYS) + si[1][0:nb, :] for a, nb in pairs]
                               + [jnp.zeros((n_pad, tm), F32)], axis=0)
        tv, te = _topk_rows(cand, TOPK, payload=cidx)
        te = te.astype(jnp.int32)
        ex = jnp.exp(tv - tv[0:1, :])
        g_rows.append(ex / jnp.sum(ex, axis=0, keepdims=True))
        i_rows.append(lax.shift_right_logical(te, 7))
        j_rows.append(te & (N_KEYS - 1))
    ii_ref[...] = jnp.concatenate(i_rows, axis=0).T
    jj_ref[...] = jnp.concatenate(j_rows, axis=0).T
    gg_ref[...] = jnp.concatenate(g_rows, axis=0).T


def _route_call(x2d, mods3, wq, keys, rows_per_mod, mod_row0, tm):
    n, d_model = x2d.shape
    tm = _tile(rows_per_mod or n, tm)
    slots = PEER_HEADS * TOPK

    def mod_idx(g):
        if rows_per_mod is None:
            return lambda i: (mod_row0 * 6 + g, 0, 0)
        return lambda i: ((mod_row0 + (i * tm) // rows_per_mod) * 6 + g, 0, 0)

    const = lambda shape: pl.BlockSpec(shape, lambda i: (0,) * len(shape), pipeline_mode=pl.Buffered(1))
    return pl.pallas_call(
        _route_kernel,
        grid=(n // tm,),
        in_specs=[
            pl.BlockSpec((tm, d_model), lambda i: (i, 0)),
            pl.BlockSpec((1, 1, d_model), mod_idx(3)),
            pl.BlockSpec((1, 1, d_model), mod_idx(4)),
            const(wq.shape), const(keys.shape),
        ],
        out_specs=[pl.BlockSpec((tm, d_model), lambda i: (i, 0))]
        + [pl.BlockSpec((tm, slots), lambda i: (i, 0))] * 3,
        out_shape=[jax.ShapeDtypeStruct((n, d_model), BF16),
                   jax.ShapeDtypeStruct((n, slots), jnp.int32),
                   jax.ShapeDtypeStruct((n, slots), jnp.int32),
                   jax.ShapeDtypeStruct((n, slots), F32)],
        compiler_params=_cparams(("parallel",)),
        name="route",
    )(x2d, mods3, mods3, wq, keys)


def _gelu_exact(x):
    return 0.5 * x * (1.0 + lax.erf(x * (2.0 ** -0.5)))


def _gates_kernel(ii_ref, jj_ref, gg_ref, o_ref, gate_s):
    tm = ii_ref.shape[0]
    sub = lax.broadcasted_iota(jnp.int32, (N_KEYS, N_KEYS), 0)

    def tokens(it, carry):
        base = pl.multiple_of(it * (GATE_UNROLL * G_PITCH), SUBLANES)
        for u in range(GATE_UNROLL):
            n = it * GATE_UNROLL + u
            ii = ii_ref[pl.ds(n, 1), :]
            jj = jj_ref[pl.ds(n, 1), :]
            gg = gg_ref[pl.ds(n, 1), :]
            a_t = jnp.where(ii == sub, gg, 0.0).astype(BF16)
            b = jnp.where(jj == sub, 1.0, 0.0).T.astype(BF16)
            gate = jnp.dot(a_t, b, preferred_element_type=F32)
            gate_s[pl.ds(base + u * G_PITCH, N_KEYS), :] = gate
        return carry

    lax.fori_loop(0, tm // GATE_UNROLL, tokens, 0)
    for i in range(N_KEYS):
        o_ref[:, i * N_KEYS:(i + 1) * N_KEYS] = gate_s[pl.ds(i, tm, stride=G_PITCH), :].astype(o_ref.dtype)


def _gates_call(ii, jj, gg, tm):
    n, slots = ii.shape
    tm = _tile(n, tm)
    tok = pl.BlockSpec((tm, slots), lambda i: (i, 0))
    return pl.pallas_call(
        _gates_kernel,
        grid=(n // tm,),
        in_specs=[tok, tok, tok],
        out_specs=pl.BlockSpec((tm, N_KEYS * N_KEYS), lambda i: (i, 0)),
        out_shape=jax.ShapeDtypeStruct((n, N_KEYS * N_KEYS), BF16),
        scratch_shapes=[pltpu.VMEM((tm * G_PITCH, LANES), F32)],
        compiler_params=_cparams(("parallel",)),
        name="gates",
    )(ii, jj, gg)


def _dense_kernel(h_ref, gate_ref, ut_ref, v_ref, x_ref, g2_ref, lng_ref, lnb_ref, o_ref, *, alpha):
    eb = pl.program_id(1)

    @pl.when(eb == 0)
    def _():
        o_ref[...] = jnp.zeros(o_ref.shape, F32)

    act = _gelu_exact(jnp.dot(h_ref[...], ut_ref[0, 0], preferred_element_type=F32))
    hg = (gate_ref[...].astype(F32) * act).astype(BF16)
    o_ref[...] += jnp.dot(hg, v_ref[0], preferred_element_type=F32)

    @pl.when(eb == pl.num_programs(1) - 1)
    def _():
        r = alpha * x_ref[...] + g2_ref[0] * o_ref[...]
        o_ref[...] = _ln_rows(r) * lng_ref[...] + lnb_ref[...]


def _dense_call(h16, gates, ut4, v16, layer, x2d, mods3, ln_g, ln_b, rows_per_mod, mod_row0, tm, alpha):
    n, d_model = x2d.shape
    tm = _tile(rows_per_mod or n, tm)
    _, n_blocks, _, te = ut4.shape

    if rows_per_mod is None:
        g2_idx = lambda i, e: (mod_row0 * 6 + 5, 0, 0)
    else:
        g2_idx = lambda i, e: ((mod_row0 + (i * tm) // rows_per_mod) * 6 + 5, 0, 0)
    tok = lambda cols: pl.BlockSpec((tm, cols), lambda i, e: (i, 0))
    const = lambda shape: pl.BlockSpec(shape, lambda i, e: (0,) * len(shape))
    return pl.pallas_call(
        functools.partial(_dense_kernel, alpha=alpha),
        grid=(n // tm, n_blocks),
        in_specs=[
            tok(d_model),
            pl.BlockSpec((tm, te), lambda i, e: (i, e)),
            pl.BlockSpec((1, 1, d_model, te), lambda i, e: (layer, e, 0, 0)),
            pl.BlockSpec((1, te, d_model), lambda i, e: (layer, e, 0)),
            tok(d_model),
            pl.BlockSpec((1, 1, d_model), g2_idx),
            const((1, d_model)), const((1, d_model)),
        ],
        out_specs=tok(d_model),
        out_shape=jax.ShapeDtypeStruct((n, d_model), F32),
        compiler_params=_cparams(("parallel", "arbitrary"), DENSE_VMEM_LIMIT),
        name="dense",
    )(h16, gates, ut4, v16, x2d, mods3, ln_g.reshape(1, d_model), ln_b.reshape(1, d_model))


def kernel(x, c, ctx, c_ctx, w_mod, b_mod, w_in, conv_w, conv_b, lb_raw, hg_norm_g, w_pa, w_pb, w_o,
           ln1_g, ln1_b, peer_wq, peer_keys, peer_u, peer_v, ln2_g, ln2_b):
    batch, s_len, d_model = x.shape
    c_len = ctx.shape[1]
    depth = w_mod.shape[0]
    alpha = (2.0 * depth) ** 0.25
    n_lat = batch * s_len
    n_ctx = batch * c_len

    p = jax.nn.softmax(lb_raw.astype(F32), axis=0)
    lower = jnp.cumsum(p, axis=0) - p[:1]

    cond = jnp.concatenate([c, c_ctx[None, :]], axis=0)
    cb = jnp.broadcast_to(cond.T[:, :, None], (d_model, batch + 1, LANES)).reshape(d_model, -1)
    mods = _mod_call(cb, w_mod, b_mod, batch + 1)

    w_in16 = _cast_call(w_in)
    v16 = _cast_call(peer_v)
    ut4 = _cast_t_call(peer_u, DENSE_TE)

    xl = x.reshape(n_lat, d_model)
    xc = ctx.reshape(n_ctx, d_model)
    for l in range(depth):
        last = l == depth - 1
        mods3 = mods[l].reshape(16 * 6, 1, d_model)
        wpa, wpb, wo = w_pa[l].astype(BF16), w_pb[l].astype(BF16), w_o[l].astype(BF16)
        wq = peer_wq[l].astype(BF16)
        keys = peer_keys[l].reshape(PEER_HEADS * 2, N_KEYS, -1).astype(BF16)

        p_lat = _inproj_call(xl, mods3, w_in16, l, s_len, 0, 1024)
        p_ctx = _inproj_call(xc, mods3, w_in16, l, None, batch, 1024)
        yb_lat, yb_ctx = _hgrn_call(p_lat, p_ctx, lower[l], hg_norm_g[l], batch, d_model)

        streams = [(xl, p_lat, yb_lat, s_len, 0, GRID_W)]
        if not last:
            streams.append((xc, p_ctx, yb_ctx, None, batch, c_len))
        outs = []
        for xs, ps, ybs, rpm, row0, row_len in streams:
            x1 = _merge_call(ps, ybs, xs, mods3, conv_w[l], conv_b[l], wpa, wpb, wo, ln1_g[l], ln1_b[l],
                             rpm, row0, row_len, 256, alpha)
            h16, ii, jj, gg = _route_call(x1, mods3, wq, keys, rpm, row0, 256)
            gates = _gates_call(ii, jj, gg, 256)
            outs.append(_dense_call(h16, gates, ut4, v16, l, x1, mods3, ln2_g[l], ln2_b[l],
                                    rpm, row0, DENSE_TM, alpha))
        xl = outs[0]
        if not last:
            xc = outs[1]
    return xl.reshape(batch, s_len, d_model)
```

```python
import functools
import math

import numpy as np
import jax
import jax.numpy as jnp
from jax import lax
from jax.experimental import pallas as pl
from jax.experimental.pallas import tpu as pltpu

F32 = jnp.float32
BF16 = jnp.bfloat16

LANES = 128
SUBLANES = 8
VMEM_LIMIT = 56 * 1024 * 1024

EPS = 1e-6
F_MIN = 1e-30
GRID_W = 64
HG_DK = 128
HG_DV = 128
CHUNK = 64
LEVELS = (32, 16, 8, 4, 2, 1, 0)
MATMUL_LEVELS = (4, 2)
INTRA_CHUNKS = 4
N_KEYS = 128
TOPK = 16
PEER_HEADS = 8
G_PITCH = 132
GATE_UNROLL = 64
MOD_TN = 512
INPROJ_TM = 1024
INPROJ_TN = 1024
MERGE_TM = 256
ROUTE_TM = 256
GATES_TM = 256
DENSE_TE = 512
DENSE_TM = 1024
DENSE_VMEM_LIMIT = 62 * 1024 * 1024
CAST_BLOCK_BYTES = 4 * 1024 * 1024


def _cparams(sem, vmem_limit=VMEM_LIMIT):
    return pltpu.CompilerParams(dimension_semantics=sem, vmem_limit_bytes=vmem_limit)


def _tile(n, pref):
    t = min(n, pref)
    while n % t:
        t //= 2
    return t


def _ln_rows(x):
    mu = jnp.mean(x, axis=-1, keepdims=True)
    xc = x - mu
    var = jnp.mean(xc * xc, axis=-1, keepdims=True)
    return xc * lax.rsqrt(var + EPS)


def _sigmoid(x):
    return 1.0 / (1.0 + jnp.exp(-x))


def _silu(x):
    return x * _sigmoid(x)


def _split_bf16(x):
    hi = x.astype(BF16)
    lo = (x - hi.astype(F32)).astype(BF16)
    return hi, lo


def _cast_kernel(x_ref, o_ref):
    o_ref[...] = x_ref[...].astype(o_ref.dtype)


def _cast_call(w):
    depth, r, c = w.shape
    rows = _tile(r, 1 << max(4, (CAST_BLOCK_BYTES // (4 * c)).bit_length() - 1))
    spec = pl.BlockSpec((1, rows, c), lambda l, i: (l, i, 0))
    return pl.pallas_call(
        _cast_kernel, grid=(depth, r // rows), in_specs=[spec], out_specs=spec,
        out_shape=jax.ShapeDtypeStruct(w.shape, BF16),
        compiler_params=_cparams(("parallel", "parallel")), name="cast",
    )(w)


def _cast_t_kernel(x_ref, o_ref):
    d = x_ref.shape[2]
    step = 256 if d % 256 == 0 else d
    for c in range(0, d, step):
        o_ref[0, 0, c:c + step, :] = x_ref[0, :, c:c + step].T.astype(o_ref.dtype)


def _cast_t_call(w, te):
    depth, e, d = w.shape
    return pl.pallas_call(
        _cast_t_kernel, grid=(depth, e // te),
        in_specs=[pl.BlockSpec((1, te, d), lambda l, i: (l, i, 0))],
        out_specs=pl.BlockSpec((1, 1, d, te), lambda l, i: (l, i, 0, 0)),
        out_shape=jax.ShapeDtypeStruct((depth, e // te, d, te), BF16),
        compiler_params=_cparams(("parallel", "parallel")), name="cast_t",
    )(w)


def _mod_kernel(cb_ref, w_ref, b_ref, o_ref, s_tab, *, n_rows):
    d_model = w_ref.shape[1]
    tn = w_ref.shape[2]

    @pl.when((pl.program_id(0) == 0) & (pl.program_id(1) == 0))
    def _():
        def fill(g, carry):
            r0 = pl.multiple_of(g * 64, 64)
            s_tab[pl.ds(r0, 64), :] = _silu(cb_ref[pl.ds(r0, 64), :])
            return carry

        lax.fori_loop(0, d_model // 64, fill, 0)

    def body(g, accs):
        r0 = pl.multiple_of(g * SUBLANES, SUBLANES)
        w8 = w_ref[0, pl.ds(r0, SUBLANES), :]
        out = []
        for r, acc in enumerate(accs):
            s8 = s_tab[pl.ds(r0, SUBLANES), r * LANES:(r + 1) * LANES]
            out.append(acc + jnp.concatenate([s8] * (tn // LANES), axis=1) * w8)
        return tuple(out)

    init = tuple(jnp.zeros((SUBLANES, tn), F32) for _ in range(n_rows))
    accs = lax.fori_loop(0, d_model // SUBLANES, body, init, unroll=2)
    bias = b_ref[0]
    o_ref[0] = jnp.zeros((o_ref.shape[1], tn), F32)
    for r, acc in enumerate(accs):
        o_ref[0, r:r + 1, :] = jnp.sum(acc, axis=0, keepdims=True) + bias


def _mod_call(cb, w_mod, b_mod, n_rows):
    depth, d_model, width = w_mod.shape
    tn = MOD_TN
    return pl.pallas_call(
        functools.partial(_mod_kernel, n_rows=n_rows),
        grid=(depth, width // tn),
        in_specs=[
            pl.BlockSpec((d_model, n_rows * LANES), lambda l, j: (0, 0), pipeline_mode=pl.Buffered(1)),
            pl.BlockSpec((1, d_model, tn), lambda l, j: (l, 0, j)),
            pl.BlockSpec((1, 1, tn), lambda l, j: (l, 0, j)),
        ],
        out_specs=pl.BlockSpec((1, 16, tn), lambda l, j: (l, 0, j)),
        out_shape=jax.ShapeDtypeStruct((depth, 16, width), F32),
        scratch_shapes=[pltpu.VMEM((d_model, n_rows * LANES), F32)],
        compiler_params=_cparams(("arbitrary", "arbitrary")),
        name="mod",
    )(cb, w_mod, b_mod.reshape(depth, 1, width))


def _inproj_kernel(x_ref, sh_ref, sc_ref, w_ref, o_ref, h_ref):
    @pl.when(pl.program_id(1) == 0)
    def _():
        y = _ln_rows(x_ref[...])
        h_ref[...] = (y * (1.0 + sc_ref[0]) + sh_ref[0]).astype(BF16)

    o_ref[...] = jnp.dot(h_ref[...], w_ref[0], preferred_element_type=F32)


def _inproj_call(x2d, mods3, w_bf16, layer, rows_per_mod, mod_row0, tm):
    n, d_model = x2d.shape
    tm = _tile(rows_per_mod or n, tm)
    width = w_bf16.shape[2]
    tn = _tile(width, INPROJ_TN)

    def mod_idx(g):
        if rows_per_mod is None:
            return lambda i, j: (mod_row0 * 6 + g, 0, 0)
        return lambda i, j: ((mod_row0 + (i * tm) // rows_per_mod) * 6 + g, 0, 0)

    return pl.pallas_call(
        _inproj_kernel,
        grid=(n // tm, width // tn),
        in_specs=[
            pl.BlockSpec((tm, d_model), lambda i, j: (i, 0)),
            pl.BlockSpec((1, 1, d_model), mod_idx(0)),
            pl.BlockSpec((1, 1, d_model), mod_idx(1)),
            pl.BlockSpec((1, d_model, tn), lambda i, j: (layer, 0, j)),
        ],
        out_specs=pl.BlockSpec((tm, tn), lambda i, j: (i, j)),
        out_shape=jax.ShapeDtypeStruct((n, width), F32),
        scratch_shapes=[pltpu.VMEM((tm, d_model), BF16)],
        compiler_params=_cparams(("parallel", "arbitrary")),
        name="inproj",
    )(x2d, mods3, mods3, w_bf16)


def _decay_matrices(reverse):
    L = CHUNK
    rank = np.arange(L)[::-1] if reverse else np.arange(L)
    rt = rank[:, None]
    rr = rank[None, :]
    blocks = [(rr <= rt)]
    for h in MATMUL_LEVELS:
        upper = (rt & h) != 0
        blocks.append(upper & (rr >= (rt & ~(h - 1))) & (rr <= rt))
        blocks.append((~upper) & (rr > rt) & (rr <= (rt | (h - 1))))
    return np.concatenate(blocks, axis=0).astype(np.float32)


def _half_block_exponents(bcum, h, reverse):
    eq, ek = [], []
    zero = jnp.zeros((h, LANES), F32)
    for b in range(CHUNK // h):
        rows = bcum[b * h:(b + 1) * h]
        first_of_pair = b % 2 == 0
        if not reverse:
            if first_of_pair:
                eq.append(zero)
                ek.append(bcum[(b + 1) * h - 1:(b + 1) * h] - rows)
            else:
                eq.append(rows - bcum[b * h - 1:b * h])
                ek.append(zero)
        else:
            if first_of_pair:
                eq.append(rows - bcum[(b + 1) * h:(b + 1) * h + 1])
                ek.append(zero)
            else:
                eq.append(zero)
                ek.append(bcum[b * h:b * h + 1] - rows)
    return jnp.concatenate(eq, axis=0), jnp.concatenate(ek, axis=0)


def _level_masks(reverse):
    L = CHUNK
    rank = np.arange(L)[::-1] if reverse else np.arange(L)
    rt = rank[:, None]
    rs = rank[None, :]
    out = []
    for h in LEVELS:
        if h == 0:
            out.append(rt == rs)
        else:
            out.append(((rt ^ rs) // h == 1) & (rt > rs))
    return np.stack(out).astype(np.float32)


def _hgrn_intra(chains):
    L = CHUNK
    gs, ks, es, bcums, blasts = [], [], [], [], []
    for qs, z, v16, lb, dmat, masks, reverse in chains:
        sig = _sigmoid(z)
        f = lb + (1.0 - lb) * sig
        g = jnp.log(jnp.maximum(f, F_MIN))
        g_hi, g_lo = _split_bf16(g)
        e = jnp.dot(dmat, jnp.concatenate([g_hi, g_lo], axis=0),
                    preferred_element_type=F32)
        gs.append(g)
        ks.append((1.0 - lb) * (1.0 - sig))
        es.append(e)
        bcums.append(e[0:L])
        blasts.append(e[0:1] if reverse else e[L - 1:L])

    scores = [jnp.zeros((L, L), F32) for _ in chains]
    for i, h in enumerate(LEVELS):
        for c, (qs, z, v16, lb, dmat, masks, reverse) in enumerate(chains):
            k = ks[c]
            if h <= 1:
                if h == 0:
                    prod = qs * k
                else:
                    prod = qs * jnp.exp(gs[c]) * pltpu.roll(k, L - 1 if reverse else 1, axis=0)
                scores[c] = scores[c] + masks[i] * jnp.sum(prod, axis=-1, keepdims=True)
                continue
            if h in MATMUL_LEVELS:
                m = 1 + 2 * MATMUL_LEVELS.index(h)
                qt = qs * jnp.exp(es[c][m * L:(m + 1) * L])
                kt = k * jnp.exp(es[c][(m + 1) * L:(m + 2) * L])
            else:
                eq, ek = _half_block_exponents(bcums[c], h, reverse)
                qt = qs * jnp.exp(eq)
                kt = k * jnp.exp(ek)
            s_h = lax.dot_general(qt.astype(BF16), kt.astype(BF16), (((1,), (1,)), ((), ())),
                                  preferred_element_type=F32)
            scores[c] = scores[c] + masks[i] * s_h

    outs = []
    for c, (qs, z, v16, lb, dmat, masks, reverse) in enumerate(chains):
        o = jnp.dot(scores[c].astype(BF16), v16, preferred_element_type=F32)
        qhat = (qs * jnp.exp(bcums[c])).astype(BF16)
        kvt = lax.dot_general(v16, (ks[c] * jnp.exp(blasts[c] - bcums[c])).astype(BF16),
                              (((0,), (0,)), ((), ())), preferred_element_type=F32)
        outs.append((o, qhat, kvt, blasts[c]))
    return outs


def _hgrn_kernel(ql, zfl, zbl, vl, ogl, qc, zfc, zbc, vc, ogc, lb_ref, ng_ref, dm_ref, mk_ref,
                 yl_ref, yc_ref, ol_s, oc_s, qh_s, kv_s, bl_s, st_s):
    lbf = lb_ref[0:1, :]
    lbb = lb_ref[1:2, :]

    def run(q_ref, zf_ref, zb_ref, v_ref, o_s, states):
        n = q_ref.shape[0] // CHUNK
        per_it = math.gcd(n, INTRA_CHUNKS)

        def intra(it, carry):
            chains = []
            for u in range(per_it):
                c = it * per_it + u
                r = pl.multiple_of(c * CHUNK, CHUNK)
                qs = _silu(q_ref[pl.ds(r, CHUNK), :]) * (HG_DK ** -0.5)
                v16 = v_ref[pl.ds(r, CHUNK), :].astype(BF16)
                chains.append((qs, zf_ref[pl.ds(r, CHUNK), :], v16, lbf, dm_ref[0], mk_ref[0], False))
                chains.append((qs, zb_ref[pl.ds(r, CHUNK), :], v16, lbb, dm_ref[1], mk_ref[1], True))
            res = _hgrn_intra(chains)
            for u in range(per_it):
                c = it * per_it + u
                r = pl.multiple_of(c * CHUNK, CHUNK)
                (of, qf, kvf, blf), (ob, qb, kvb, blb) = res[2 * u], res[2 * u + 1]
                o_s[pl.ds(r, CHUNK), :] = of + ob
                qh_s[pl.ds(r, CHUNK), 0:HG_DK] = qf
                qh_s[pl.ds(r, CHUNK), HG_DK:2 * HG_DK] = qb
                kv_s[0, c] = kvf
                kv_s[1, c] = kvb
                bl_s[0, c] = jnp.broadcast_to(blf, (SUBLANES, HG_DK))
                bl_s[1, c] = jnp.broadcast_to(blb, (SUBLANES, HG_DK))
            return carry

        lax.fori_loop(0, n // per_it, intra, 0)

        def recur(c, carry):
            sf, sb = carry
            cb = n - 1 - c
            st_s[c, :, 0:HG_DK] = sf.astype(BF16)
            st_s[cb, :, HG_DK:2 * HG_DK] = sb.astype(BF16)
            sf = sf * jnp.exp(bl_s[0, c][0:1, :]) + kv_s[0, c]
            sb = sb * jnp.exp(bl_s[1, cb][0:1, :]) + kv_s[1, cb]
            return sf, sb

        states = lax.fori_loop(0, n, recur, states)

        def inter(c, carry):
            r = pl.multiple_of(c * CHUNK, CHUNK)
            o_s[pl.ds(r, CHUNK), :] += lax.dot_general(
                qh_s[pl.ds(r, CHUNK), :], st_s[c], (((1,), (1,)), ((), ())), preferred_element_type=F32)
            return carry

        lax.fori_loop(0, n, inter, 0, unroll=4)
        return states

    zero = jnp.zeros((HG_DV, HG_DK), F32)
    states = run(qc, zfc, zbc, vc, oc_s, (zero, zero))
    run(ql, zfl, zbl, vl, ol_s, states)

    def readout(o_s, og_ref, y_ref):
        o = o_s[...]
        o = o * lax.rsqrt(jnp.mean(o * o, axis=-1, keepdims=True) + EPS) * ng_ref[...]
        y_ref[...] = (o * _silu(og_ref[...])).astype(y_ref.dtype)

    readout(ol_s, ogl, yl_ref)
    readout(oc_s, ogc, yc_ref)


def _hgrn_call(p_lat, p_ctx, lb, norm_g, batch, d_model):
    n_lat = p_lat.shape[0]
    n_ctx = p_ctx.shape[0]
    s_len = n_lat // batch
    c_len = n_ctx // batch
    d_half = d_model // 2
    heads = d_half // HG_DK
    cb = d_half // LANES
    q0, zf0, zb0, v0, og0 = 3 * cb, 4 * cb, 5 * cb, 6 * cb, 7 * cb

    def spec(rows, c0):
        return pl.BlockSpec((rows, LANES), lambda b, h, c0=c0: (b, c0 + h))

    dmat = jnp.asarray(np.stack([np.tile(_decay_matrices(r), (1, 2)) for r in (False, True)]), BF16)
    masks = jnp.asarray(np.stack([_level_masks(False), _level_masks(True)]), F32)
    full = lambda shape: pl.BlockSpec(shape, lambda b, h: (0,) * len(shape))

    return pl.pallas_call(
        _hgrn_kernel,
        grid=(batch, heads),
        in_specs=[spec(s_len, c0) for c0 in (q0, zf0, zb0, v0, og0)]
        + [spec(c_len, c0) for c0 in (q0, zf0, zb0, v0, og0)]
        + [pl.BlockSpec((2, LANES), lambda b, h: (0, h)),
           full((1, HG_DV)), full(dmat.shape), full(masks.shape)],
        out_specs=[pl.BlockSpec((s_len, LANES), lambda b, h: (b, h)),
                   pl.BlockSpec((c_len, LANES), lambda b, h: (b, h))],
        out_shape=[jax.ShapeDtypeStruct((n_lat, d_half), BF16),
                   jax.ShapeDtypeStruct((n_ctx, d_half), BF16)],
        scratch_shapes=[pltpu.VMEM((s_len, HG_DV), F32), pltpu.VMEM((c_len, HG_DV), F32),
                        pltpu.VMEM((s_len, 2 * HG_DK), BF16),
                        pltpu.VMEM((2, s_len // CHUNK, HG_DV, HG_DK), F32),
                        pltpu.VMEM((2, s_len // CHUNK, SUBLANES, HG_DK), F32),
                        pltpu.VMEM((s_len // CHUNK, HG_DV, 2 * HG_DK), BF16)],
        compiler_params=_cparams(("parallel", "parallel")),
        name="hgrn",
    )(*([p_lat] * 5), *([p_ctx] * 5), lb, norm_g.reshape(1, HG_DV), dmat, masks)


def _merge_kernel(cb_ref, cc_ref, cv_ref, ga_ref, gb_ref, yb_ref, x_ref, g1_ref, cw_ref, cbias_ref,
                  wpa_ref, wpb_ref, wo_ref, lng_ref, lnb_ref, o_ref, *, row_len, alpha):
    tm = x_ref.shape[0]
    u = cc_ref[...] * cv_ref[...]
    pos = lax.broadcasted_iota(jnp.int32, (tm, 1), 0) % row_len
    prev = jnp.where(pos == 0, 0.0, pltpu.roll(u, 1, axis=0))
    nxt = jnp.where(pos == row_len - 1, 0.0, pltpu.roll(u, tm - 1, axis=0))
    conv = prev * cw_ref[0:1, :] + u * cw_ref[1:2, :] + nxt * cw_ref[2:3, :] + cbias_ref[...]
    ya = (cb_ref[...] * conv).astype(BF16)
    m = (_sigmoid(ga_ref[...]) * jnp.dot(ya, wpa_ref[...], preferred_element_type=F32)
         + _sigmoid(gb_ref[...]) * jnp.dot(yb_ref[...], wpb_ref[...], preferred_element_type=F32))
    y = jnp.dot(m.astype(BF16), wo_ref[...], preferred_element_type=F32)
    r = alpha * x_ref[...] + g1_ref[0] * y
    o_ref[...] = _ln_rows(r) * lng_ref[...] + lnb_ref[...]


def _merge_call(p, yb, x2d, mods3, conv_w, conv_b, wpa, wpb, wo, ln_g, ln_b, rows_per_mod, mod_row0,
                row_len, tm, alpha):
    n, d_model = x2d.shape
    tm = _tile(rows_per_mod or n, tm)
    d_half = d_model // 2

    if rows_per_mod is None:
        g1_idx = lambda i: (mod_row0 * 6 + 2, 0, 0)
    else:
        g1_idx = lambda i: ((mod_row0 + (i * tm) // rows_per_mod) * 6 + 2, 0, 0)
    const = lambda shape: pl.BlockSpec(shape, lambda i: (0,) * len(shape), pipeline_mode=pl.Buffered(1))
    return pl.pallas_call(
        functools.partial(_merge_kernel, row_len=row_len, alpha=alpha),
        grid=(n // tm,),
        in_specs=[
            pl.BlockSpec((tm, d_half), lambda i: (i, 0)),
            pl.BlockSpec((tm, d_half), lambda i: (i, 1)),
            pl.BlockSpec((tm, d_half), lambda i: (i, 2)),
            pl.BlockSpec((tm, d_model), lambda i: (i, 4)),
            pl.BlockSpec((tm, d_model), lambda i: (i, 5)),
            pl.BlockSpec((tm, d_half), lambda i: (i, 0)),
            pl.BlockSpec((tm, d_model), lambda i: (i, 0)),
            pl.BlockSpec((1, 1, d_model), g1_idx),
            const((3, d_half)), const((1, d_half)),
            const((d_half, d_model)), const((d_half, d_model)), const((d_model, d_model)),
            const((1, d_model)), const((1, d_model)),
        ],
        out_specs=pl.BlockSpec((tm, d_model), lambda i: (i, 0)),
        out_shape=jax.ShapeDtypeStruct((n, d_model), F32),
        compiler_params=_cparams(("parallel",)),
        name="merge",
    )(p, p, p, p, p, yb, x2d, mods3, conv_w, conv_b.reshape(1, d_half), wpa, wpb, wo,
      ln_g.reshape(1, d_model), ln_b.reshape(1, d_model))


def _topk_rows(s, k, payload=None):
    rows = s.shape[0]
    riota = lax.broadcasted_iota(jnp.int32, s.shape, 0).astype(F32)
    vals, picks = [], []
    for _ in range(k):
        m = jnp.max(s, axis=0, keepdims=True)
        am = jnp.min(jnp.where(s == m, riota, float(rows)), axis=0, keepdims=True)
        sel = riota == am
        vals.append(m)
        if payload is None:
            picks.append(am)
        else:
            picks.append(jnp.max(jnp.where(sel, payload, -1.0), axis=0, keepdims=True))
        s = jnp.where(sel, -jnp.inf, s)
    return jnp.concatenate(vals, axis=0), jnp.concatenate(picks, axis=0)


def _pair_candidates():
    return [(a, min(TOPK, TOPK // (a + 1))) for a in range(TOPK)]


def _route_kernel(x_ref, sh_ref, sc_ref, wq_ref, keys_ref, h_ref, ii_ref, jj_ref, gg_ref):
    hmod = _ln_rows(x_ref[...]) * (1.0 + sc_ref[0]) + sh_ref[0]
    h16 = hmod.astype(BF16)
    h_ref[...] = h16
    qp = jnp.dot(h16, wq_ref[...], preferred_element_type=F32).astype(BF16)
    i_rows, j_rows, g_rows = [], [], []
    for head in range(PEER_HEADS):
        sv, si = [], []
        for half in range(2):
            c = (head * 2 + half) * LANES
            s = lax.dot_general(keys_ref[head * 2 + half], qp[:, c:c + LANES], (((1,), (1,)), ((), ())),
                                preferred_element_type=F32)
            v_, i_ = _topk_rows(s, TOPK)
            sv.append(v_)
            si.append(i_)
        pairs = _pair_candidates()
        n_pad = -sum(nb for _, nb in pairs) % SUBLANES
        tm = s.shape[1]
        cand = jnp.concatenate([sv[0][a:a + 1, :] + sv[1][0:nb, :] for a, nb in pairs]
                               + [jnp.full((n_pad, tm), -jnp.inf, F32)], axis=0)
        cidx = jnp.concatenate([si[0][a:a + 1, :] * float(N_KEYS) + si[1][0:nb, :] for a, nb in pairs]
                               + [jnp.zeros((n_pad, tm), F32)], axis=0)
        tv, te = _topk_rows(cand, TOPK, payload=cidx)
        te = te.astype(jnp.int32)
        ex = jnp.exp(tv - tv[0:1, :])
        g_rows.append(ex / jnp.sum(ex, axis=0, keepdims=True))
        i_rows.append(lax.shift_right_logical(te, 7))
        j_rows.append(te & (N_KEYS - 1))
    ii_ref[...] = jnp.concatenate(i_rows, axis=0).T
    jj_ref[...] = jnp.concatenate(j_rows, axis=0).T
    gg_ref[...] = jnp.concatenate(g_rows, axis=0).T


def _route_call(x2d, mods3, wq, keys, rows_per_mod, mod_row0, tm):
    n, d_model = x2d.shape
    tm = _tile(rows_per_mod or n, tm)
    slots = PEER_HEADS * TOPK

    def mod_idx(g):
        if rows_per_mod is None:
            return lambda i: (mod_row0 * 6 + g, 0, 0)
        return lambda i: ((mod_row0 + (i * tm) // rows_per_mod) * 6 + g, 0, 0)

    const = lambda shape: pl.BlockSpec(shape, lambda i: (0,) * len(shape), pipeline_mode=pl.Buffered(1))
    return pl.pallas_call(
        _route_kernel,
        grid=(n // tm,),
        in_specs=[
            pl.BlockSpec((tm, d_model), lambda i: (i, 0)),
            pl.BlockSpec((1, 1, d_model), mod_idx(3)),
            pl.BlockSpec((1, 1, d_model), mod_idx(4)),
            const(wq.shape), const(keys.shape),
        ],
        out_specs=[pl.BlockSpec((tm, d_model), lambda i: (i, 0))]
        + [pl.BlockSpec((tm, slots), lambda i: (i, 0))] * 3,
        out_shape=[jax.ShapeDtypeStruct((n, d_model), BF16),
                   jax.ShapeDtypeStruct((n, slots), jnp.int32),
                   jax.ShapeDtypeStruct((n, slots), jnp.int32),
                   jax.ShapeDtypeStruct((n, slots), F32)],
        compiler_params=_cparams(("parallel",)),
        name="route",
    )(x2d, mods3, mods3, wq, keys)


def _gelu_exact(x):
    return 0.5 * x * (1.0 + lax.erf(x * (2.0 ** -0.5)))


def _gates_kernel(ii_ref, jj_ref, gg_ref, o_ref, gate_s):
    tm = ii_ref.shape[0]
    sub = lax.broadcasted_iota(jnp.int32, (N_KEYS, N_KEYS), 0)

    def tokens(it, carry):
        base = pl.multiple_of(it * (GATE_UNROLL * G_PITCH), SUBLANES)
        for u in range(GATE_UNROLL):
            n = it * GATE_UNROLL + u
            ii = ii_ref[pl.ds(n, 1), :]
            jj = jj_ref[pl.ds(n, 1), :]
            gg = gg_ref[pl.ds(n, 1), :]
            a_t = jnp.where(ii == sub, gg, 0.0).astype(BF16)
            b = jnp.where(jj == sub, 1.0, 0.0).T.astype(BF16)
            gate = jnp.dot(a_t, b, preferred_element_type=F32)
            gate_s[pl.ds(base + u * G_PITCH, N_KEYS), :] = gate
        return carry

    lax.fori_loop(0, tm // GATE_UNROLL, tokens, 0)
    for i in range(N_KEYS):
        o_ref[:, i * N_KEYS:(i + 1) * N_KEYS] = gate_s[pl.ds(i, tm, stride=G_PITCH), :].astype(o_ref.dtype)


def _gates_call(ii, jj, gg, tm):
    n, slots = ii.shape
    tm = _tile(n, tm)
    tok = pl.BlockSpec((tm, slots), lambda i: (i, 0))
    return pl.pallas_call(
        _gates_kernel,
        grid=(n // tm,),
        in_specs=[tok, tok, tok],
        out_specs=pl.BlockSpec((tm, N_KEYS * N_KEYS), lambda i: (i, 0)),
        out_shape=jax.ShapeDtypeStruct((n, N_KEYS * N_KEYS), BF16),
        scratch_shapes=[pltpu.VMEM((tm * G_PITCH, LANES), F32)],
        compiler_params=_cparams(("parallel",)),
        name="gates",
    )(ii, jj, gg)


def _dense_kernel(h_ref, gate_ref, ut_ref, v_ref, x_ref, g2_ref, lng_ref, lnb_ref, o_ref, *, alpha):
    eb = pl.program_id(1)

    @pl.when(eb == 0)
    def _():
        o_ref[...] = jnp.zeros(o_ref.shape, F32)

    act = _gelu_exact(jnp.dot(h_ref[...], ut_ref[0, 0], preferred_element_type=F32))
    hg = (gate_ref[...].astype(F32) * act).astype(BF16)
    o_ref[...] += jnp.dot(hg, v_ref[0], preferred_element_type=F32)

    @pl.when(eb == pl.num_programs(1) - 1)
    def _():
        r = alpha * x_ref[...] + g2_ref[0] * o_ref[...]
        o_ref[...] = _ln_rows(r) * lng_ref[...] + lnb_ref[...]


def _dense_call(h16, gates, ut4, v16, layer, x2d, mods3, ln_g, ln_b, rows_per_mod, mod_row0, tm, alpha):
    n, d_model = x2d.shape
    tm = _tile(rows_per_mod or n, tm)
    _, n_blocks, _, te = ut4.shape

    if rows_per_mod is None:
        g2_idx = lambda i, e: (mod_row0 * 6 + 5, 0, 0)
    else:
        g2_idx = lambda i, e: ((mod_row0 + (i * tm) // rows_per_mod) * 6 + 5, 0, 0)
    tok = lambda cols: pl.BlockSpec((tm, cols), lambda i, e: (i, 0))
    const = lambda shape: pl.BlockSpec(shape, lambda i, e: (0,) * len(shape))
    return pl.pallas_call(
        functools.partial(_dense_kernel, alpha=alpha),
        grid=(n // tm, n_blocks),
        in_specs=[
            tok(d_model),
            pl.BlockSpec((tm, te), lambda i, e: (i, e)),
            pl.BlockSpec((1, 1, d_model, te), lambda i, e: (layer, e, 0, 0)),
            pl.BlockSpec((1, te, d_model), lambda i, e: (layer, e, 0)),
            tok(d_model),
            pl.BlockSpec((1, 1, d_model), g2_idx),
            const((1, d_model)), const((1, d_model)),
        ],
        out_specs=tok(d_model),
        out_shape=jax.ShapeDtypeStruct((n, d_model), F32),
        compiler_params=_cparams(("parallel", "arbitrary"), DENSE_VMEM_LIMIT),
        name="dense",
    )(h16, gates, ut4, v16, x2d, mods3, ln_g.reshape(1, d_model), ln_b.reshape(1, d_model))


def kernel(x, c, ctx, c_ctx, w_mod, b_mod, w_in, conv_w, conv_b, lb_raw, hg_norm_g, w_pa, w_pb, w_o,
           ln1_g, ln1_b, peer_wq, peer_keys, peer_u, peer_v, ln2_g, ln2_b):
    batch, s_len, d_model = x.shape
    c_len = ctx.shape[1]
    depth = w_mod.shape[0]
    alpha = (2.0 * depth) ** 0.25
    n_lat = batch * s_len
    n_ctx = batch * c_len

    p = jax.nn.softmax(lb_raw.astype(F32), axis=0)
    lower = jnp.cumsum(p, axis=0) - p[:1]

    cond = jnp.concatenate([c, c_ctx[None, :]], axis=0)
    cb = jnp.broadcast_to(cond.T[:, :, None], (d_model, batch + 1, LANES)).reshape(d_model, -1)
    mods = _mod_call(cb, w_mod, b_mod, batch + 1)

    w_in16 = _cast_call(w_in)
    v16 = _cast_call(peer_v)
    ut4 = _cast_t_call(peer_u, DENSE_TE)

    xl = x.reshape(n_lat, d_model)
    xc = ctx.reshape(n_ctx, d_model)
    for l in range(depth):
        last = l == depth - 1
        mods3 = mods[l].reshape(16 * 6, 1, d_model)
        wpa, wpb, wo = w_pa[l].astype(BF16), w_pb[l].astype(BF16), w_o[l].astype(BF16)
        wq = peer_wq[l].astype(BF16)
        keys = peer_keys[l].reshape(PEER_HEADS * 2, N_KEYS, -1).astype(BF16)

        p_lat = _inproj_call(xl, mods3, w_in16, l, s_len, 0, INPROJ_TM)
        p_ctx = _inproj_call(xc, mods3, w_in16, l, None, batch, INPROJ_TM)
        yb_lat, yb_ctx = _hgrn_call(p_lat, p_ctx, lower[l], hg_norm_g[l], batch, d_model)

        streams = [(xl, p_lat, yb_lat, s_len, 0, GRID_W)]
        if not last:
            streams.append((xc, p_ctx, yb_ctx, None, batch, c_len))
        outs = []
        for xs, ps, ybs, rpm, row0, row_len in streams:
            x1 = _merge_call(ps, ybs, xs, mods3, conv_w[l], conv_b[l], wpa, wpb, wo, ln1_g[l], ln1_b[l],
                             rpm, row0, row_len, MERGE_TM, alpha)
            h16, ii, jj, gg = _route_call(x1, mods3, wq, keys, rpm, row0, ROUTE_TM)
            gates = _gates_call(ii, jj, gg, GATES_TM)
            outs.append(_dense_call(h16, gates, ut4, v16, l, x1, mods3, ln2_g[l], ln2_b[l],
                                    rpm, row0, DENSE_TM, alpha))
        xl = outs[0]
        if not last:
            xc = outs[1]
    return xl.reshape(batch, s_len, d_model)
```
